```python
import math
import jax, jax.numpy as jnp
from jax import lax
import numpy as np

D_MODEL = 1024
BATCH = 16
SEQ = 4096
DEPTH = 2
DEC_BATCH = 32
DEC_SEQ = 2048
PAST_LEN = 128

ATT_HEADS = 4
ATT_QK_DIM = 64
ATT_V_DIM = 128
ATT_WIDTH = ATT_HEADS * ATT_V_DIM
LSTM_HEADS = 4
LSTM_QK_DIM = 64
LSTM_V_DIM = 128
LSTM_WIDTH = LSTM_HEADS * LSTM_V_DIM
MIX_WIDTH = ATT_WIDTH + LSTM_WIDTH
N_GATES = 4 * LSTM_HEADS
COL_WIDTHS = (ATT_HEADS * 2 * ATT_QK_DIM,
              ATT_HEADS * 2 * ATT_QK_DIM,
              ATT_WIDTH,
              LSTM_HEADS * LSTM_QK_DIM,
              LSTM_HEADS * LSTM_QK_DIM,
              LSTM_WIDTH,
              LSTM_WIDTH,
              N_GATES)
IN_WIDTH = 512 + 512 + 512 + 256 + 256 + 512 + 512 + 16
V_COLS = ((1024, 1536), (2048, 2560))
D_FF = 2816
CONV_WIDTH = 3
Q_BLOCK = 128
CHUNK = 128
ALPHA = (2 * DEPTH) ** 0.25
BETA = (8 * DEPTH) ** -0.25
EPS = 1e-5

kernel_name = "hybrid_diffattn_mlstm_encoder"


def _split_cols(proj):
    out, start = [], 0
    for w in COL_WIDTHS:
        out.append(proj[..., start:start + w])
        start += w
    return out


def _layer_norm(x, g, b):
    xf = x.astype(jnp.float32)
    mu = jnp.mean(xf, -1, keepdims=True)
    var = jnp.mean(jnp.square(xf - mu), -1, keepdims=True)
    y = (xf - mu) * lax.rsqrt(var + EPS) * g.astype(jnp.float32) + b.astype(jnp.float32)
    return y.astype(x.dtype)


def _head_rmsnorm(h, gain):
    B, S, H, dv = h.shape
    hf = h.astype(jnp.float32)
    hf = hf * lax.rsqrt(jnp.mean(hf * hf, -1, keepdims=True) + EPS)
    return hf.reshape(B, S, H * dv) * gain.astype(jnp.float32)


def _alibi_slopes(n):
    return jnp.power(2.0, -8.0 * jnp.arange(1, n + 1, dtype=jnp.float32) / n)


def _diff_attention(q, k, v, lam):
    B, S, H, _, dk = q.shape
    dv = v.shape[-1]
    nb = S // Q_BLOCK
    qb = (q * (dk ** -0.5)).reshape(B, nb, Q_BLOCK, H, 2, dk).transpose(1, 0, 2, 3, 4, 5)
    pos_k = jnp.arange(S, dtype=jnp.float32)
    slopes = _alibi_slopes(H)

    def block(args):
        qi, start = args
        pos_q = start + jnp.arange(Q_BLOCK, dtype=jnp.float32)
        bias = -slopes[:, None, None] * jnp.abs(pos_q[:, None] - pos_k[None, :])
        s = jnp.einsum('bqhcd,bshcd->bhcqs', qi, k, preferred_element_type=jnp.float32)
        p = jax.nn.softmax(s + bias[None, :, None], axis=-1)
        a = p[:, :, 0] - lam * p[:, :, 1]
        return jnp.einsum('bhqs,bshd->bqhd', a.astype(v.dtype), v)

    starts = jnp.arange(nb, dtype=jnp.float32) * Q_BLOCK
    out = lax.map(block, (qb, starts))
    return out.transpose(1, 0, 2, 3, 4).reshape(B, S, H, dv)


def _mlstm_direction(q, k, v, ig, lf):
    B, H, S, dk = q.shape
    dv = v.shape[-1]
    L = CHUNK
    nc = S // L
    qc = q.reshape(B, H, nc, L, dk)
    kc = k.reshape(B, H, nc, L, dk)
    vc = v.reshape(B, H, nc, L, dv)
    igc = ig.reshape(B, H, nc, L)
    b = jnp.cumsum(lf.reshape(B, H, nc, L), axis=-1)
    g = b[..., -1]
    tri = jnp.tril(jnp.ones((L, L), dtype=bool))
    D = jnp.where(tri, b[..., :, None] - b[..., None, :] + igc[..., None, :], -jnp.inf)
    w_end = g[..., None] - b + igc
    m_loc = jnp.max(w_end, -1)
    e_end = jnp.exp(w_end - m_loc[..., None])
    ke = kc.astype(jnp.float32) * e_end[..., None]
    C_loc = jnp.einsum('bhcld,bhcle->bhcde', ke, vc.astype(jnp.float32))
    n_loc = jnp.sum(ke, axis=3)

    def step(carry, inp):
        C, n, m = carry
        Cl, nl, ml, gc = inp
        m_new = jnp.maximum(gc + m, ml)
        a = jnp.exp(gc + m - m_new)
        bb = jnp.exp(ml - m_new)
        C_new = a[..., None, None] * C + bb[..., None, None] * Cl
        n_new = a[..., None] * n + bb[..., None] * nl
        return (C_new, n_new, m_new), (C, n, m)

    init = (jnp.zeros((B, H, dk, dv), jnp.float32), jnp.zeros((B, H, dk), jnp.float32),
            jnp.zeros((B, H), jnp.float32))
    xs = (jnp.moveaxis(C_loc, 2, 0), jnp.moveaxis(n_loc, 2, 0),
          jnp.moveaxis(m_loc, 2, 0), jnp.moveaxis(g, 2, 0))
    _, (C_prev, n_prev, m_prev) = lax.scan(step, init, xs)
    C_prev = jnp.moveaxis(C_prev, 0, 2)
    n_prev = jnp.moveaxis(n_prev, 0, 2)
    m_prev = jnp.moveaxis(m_prev, 0, 2)

    inter_log = b + m_prev[..., None]
    m_out = jnp.maximum(inter_log, jnp.max(D, -1))
    e_inter = jnp.exp(inter_log - m_out)
    P = jnp.exp(D - m_out[..., None])
    qf = qc.astype(jnp.float32)
    sqk = jnp.einsum('bhcjd,bhcsd->bhcjs', qf, kc.astype(jnp.float32)) * P
    num = (e_inter[..., None] * jnp.einsum('bhcjd,bhcde->bhcje', qf, C_prev)
           + jnp.einsum('bhcjs,bhcse->bhcje', sqk, vc.astype(jnp.float32)))
    den = e_inter * jnp.einsum('bhcjd,bhcd->bhcj', qf, n_prev) + jnp.sum(sqk, -1)
    h = num / jnp.maximum(jnp.abs(den), jnp.exp(-m_out))[..., None]
    return h.reshape(B, H, S, dv)


def _mixer(x, w_in, gate_bias, lam_q1, lam_k1, lam_q2, lam_k2, att_g, lstm_g, w_out, lam_init):
    B, S, _ = x.shape
    aq, ak, av, lq, lk, lv, lo, lg = _split_cols(x @ w_in)
    lam = (jnp.exp(jnp.sum(lam_q1.astype(jnp.float32) * lam_k1.astype(jnp.float32)))
           - jnp.exp(jnp.sum(lam_q2.astype(jnp.float32) * lam_k2.astype(jnp.float32))) + lam_init)
    att = _diff_attention(aq.reshape(B, S, ATT_HEADS, 2, ATT_QK_DIM),
                          ak.reshape(B, S, ATT_HEADS, 2, ATT_QK_DIM),
                          av.reshape(B, S, ATT_HEADS, ATT_V_DIM), lam)
    att = _head_rmsnorm(att, att_g) * (1.0 - lam_init)
    q = lq.reshape(B, S, LSTM_HEADS, LSTM_QK_DIM).transpose(0, 2, 1, 3)
    k = (lk.reshape(B, S, LSTM_HEADS, LSTM_QK_DIM) * (LSTM_QK_DIM ** -0.5)).transpose(0, 2, 1, 3)
    v = lv.reshape(B, S, LSTM_HEADS, LSTM_V_DIM).transpose(0, 2, 1, 3)
    gates = (lg.astype(jnp.float32) + gate_bias.astype(jnp.float32)).reshape(B, S, 4, LSTM_HEADS)
    gates = gates.transpose(2, 0, 3, 1)
    ig_f, lf_f = gates[0], jax.nn.log_sigmoid(gates[1])
    ig_b, lf_b = gates[2], jax.nn.log_sigmoid(gates[3])
    h_f = _mlstm_direction(q, k, v, ig_f, lf_f)
    fl = lambda t: jnp.flip(t, axis=2)
    h_b = fl(_mlstm_direction(fl(q), fl(k), fl(v), fl(ig_b), fl(lf_b)))
    h = (h_f + h_b).transpose(0, 2, 1, 3)
    lstm = jax.nn.sigmoid(lo.astype(jnp.float32)) * _head_rmsnorm(h, lstm_g)
    mixed = jnp.concatenate([att, lstm], axis=-1).astype(x.dtype)
    return mixed @ w_out


def _conv_ffn(x, w_gu, conv_w, conv_b, w_down):
    S = x.shape[1]
    gu = x @ w_gu
    gate, up = gu[..., :D_FF], gu[..., D_FF:]
    half = CONV_WIDTH // 2
    gp = jnp.pad(gate, ((0, 0), (half, half), (0, 0)))
    conv = conv_b.astype(gate.dtype) + sum(gp[:, j:j + S] * conv_w[j] for j in range(CONV_WIDTH))
    hmid = jax.nn.gelu(conv.astype(jnp.float32), approximate=False) * up.astype(jnp.float32)
    return hmid.astype(x.dtype) @ w_down


def _trunk(x, w_in, gate_bias, lam_q1, lam_k1, lam_q2, lam_k2, att_norm_g, lstm_norm_g, w_out,
           ln1_g, ln1_b, w_gu, conv_w, conv_b, w_down, ln2_g, ln2_b):
    for l in range(DEPTH):
        lam_init = 0.8 - 0.6 * math.exp(-0.3 * l)
        mix = _mixer(x, w_in[l], gate_bias[l], lam_q1[l], lam_k1[l], lam_q2[l], lam_k2[l],
                     att_norm_g[l], lstm_norm_g[l], w_out[l], lam_init)
        x = _layer_norm(ALPHA * x + mix.astype(x.dtype), ln1_g[l], ln1_b[l])
        ffn = _conv_ffn(x, w_gu[l], conv_w[l], conv_b[l], w_down[l])
        x = _layer_norm(ALPHA * x + ffn.astype(x.dtype), ln2_g[l], ln2_b[l])
    return x


def setup_inputs(seed: int = 0) -> dict:
    key = jax.random.key(seed)
    ks = jax.random.split(key, 20)
    f32 = jnp.float32
    nrm = lambda k, shape: jax.random.normal(k, shape, f32)
    col = jnp.arange(IN_WIDTH)
    v_mask = ((col >= V_COLS[0][0]) & (col < V_COLS[0][1])) | ((col >= V_COLS[1][0]) & (col < V_COLS[1][1]))
    col_scale = jnp.where(v_mask, BETA, 1.0).astype(f32)
    w_in = nrm(ks[2], (DEPTH, D_MODEL, IN_WIDTH)) * (D_MODEL ** -0.5) * col_scale
    f_off = jnp.linspace(3.0, 6.0, LSTM_HEADS, dtype=f32)
    zero_h = jnp.zeros((LSTM_HEADS,), f32)
    gate_off = jnp.concatenate([zero_h, f_off, zero_h, f_off])
    gate_bias = gate_off + 0.1 * nrm(ks[3], (DEPTH, N_GATES))
    return {
        "x_prompt": nrm(ks[0], (BATCH, SEQ, D_MODEL)),
        "x_sample": nrm(ks[1], (DEC_BATCH, DEC_SEQ, D_MODEL)),
        "w_in": w_in,
        "gate_bias": gate_bias,
        "lam_q1": 0.1 * nrm(ks[4], (DEPTH, ATT_QK_DIM)),
        "lam_k1": 0.1 * nrm(ks[5], (DEPTH, ATT_QK_DIM)),
        "lam_q2": 0.1 * nrm(ks[6], (DEPTH, ATT_QK_DIM)),
        "lam_k2": 0.1 * nrm(ks[7], (DEPTH, ATT_QK_DIM)),
        "att_norm_g": 1.0 + 0.02 * nrm(ks[8], (DEPTH, ATT_WIDTH)),
        "lstm_norm_g": 1.0 + 0.02 * nrm(ks[9], (DEPTH, LSTM_WIDTH)),
        "w_out": nrm(ks[10], (DEPTH, MIX_WIDTH, D_MODEL)) * (MIX_WIDTH ** -0.5) * BETA,
        "ln1_g": 1.0 + 0.02 * nrm(ks[11], (DEPTH, D_MODEL)),
        "ln1_b": 0.02 * nrm(ks[12], (DEPTH, D_MODEL)),
        "w_gu": nrm(ks[13], (DEPTH, D_MODEL, 2 * D_FF)) * (D_MODEL ** -0.5) * BETA,
        "conv_w": nrm(ks[14], (DEPTH, CONV_WIDTH, D_FF)) * (CONV_WIDTH ** -0.5),
        "conv_b": 0.02 * nrm(ks[15], (DEPTH, D_FF)),
        "w_down": nrm(ks[16], (DEPTH, D_FF, D_MODEL)) * (D_FF ** -0.5) * BETA,
        "ln2_g": 1.0 + 0.02 * nrm(ks[17], (DEPTH, D_MODEL)),
        "ln2_b": 0.02 * nrm(ks[18], (DEPTH, D_MODEL)),
    }


def reference(x_prompt, x_sample, w_in, gate_bias, lam_q1, lam_k1, lam_q2, lam_k2, att_norm_g,
              lstm_norm_g, w_out, ln1_g, ln1_b, w_gu, conv_w, conv_b, w_down, ln2_g, ln2_b):
    y_prompt = _trunk(x_prompt, w_in, gate_bias, lam_q1, lam_k1, lam_q2, lam_k2, att_norm_g, lstm_norm_g,
                      w_out, ln1_g, ln1_b, w_gu, conv_w, conv_b, w_down, ln2_g, ln2_b)
    y_sample = _trunk(x_sample, w_in, gate_bias, lam_q1, lam_k1, lam_q2, lam_k2, att_norm_g, lstm_norm_g,
                      w_out, ln1_g, ln1_b, w_gu, conv_w, conv_b, w_down, ln2_g, ln2_b)
    return (y_prompt, y_sample)
```

```python
import functools
import math

import jax
import jax.numpy as jnp
from jax import lax
from jax.experimental import pallas as pl
from jax.experimental.pallas import tpu as pltpu

F32 = jnp.float32
BF16 = jnp.bfloat16

D_MODEL = 1024
DEPTH = 2
HEADS = 4
QK_DIM = 64
V_DIM = 128
ATT_WIDTH = HEADS * V_DIM
LSTM_WIDTH = HEADS * V_DIM
N_GATES = 4 * HEADS
D_FF = 2816
CHUNK = 128
ALPHA = (2 * DEPTH) ** 0.25
EPS = 1e-5
QK_SCALE = QK_DIM ** -0.5

C_AQ, C_AK, C_AV, C_LQ, C_LK, C_LV, C_LO, C_LG, C_END = 0, 512, 1024, 1536, 1792, 2048, 2560, 3072, 3088

LANES = 128
MXU_TILE = 256
VMEM_LIMIT = 56 * 1024 * 1024

TOKEN_BLOCK = 512
FF_BLOCK = 1408
Q_BLOCK = MXU_TILE
K_TILE = MXU_TILE

NT_DIMS = (((1,), (1,)), ((), ()))
TN_DIMS = (((0,), (0,)), ((), ()))


def _params(sem):
    return pltpu.CompilerParams(dimension_semantics=sem, vmem_limit_bytes=VMEM_LIMIT)


def _layer_norm(z, g, b):
    mu = jnp.mean(z, axis=-1, keepdims=True)
    zc = z - mu
    var = jnp.mean(zc * zc, axis=-1, keepdims=True)
    return zc * lax.rsqrt(var + EPS) * g + b


def _inproj_kernel(x_ref, w_ref, wkt_ref, wgt_ref, gb_ref,
                   aq_ref, ak_ref, av_ref, lq_ref, lkt_ref, lv_ref, lo_ref, lgt_ref):
    xb = x_ref[...].astype(BF16)

    def seg(a, b):
        return jnp.dot(xb, w_ref[:, a:b], preferred_element_type=F32)

    aq_ref[...] = (seg(C_AQ, C_AK) * QK_SCALE).astype(BF16)
    ak_ref[...] = seg(C_AK, C_AV).astype(BF16)
    av_ref[...] = seg(C_AV, C_LQ).astype(BF16)
    lq_ref[...] = seg(C_LQ, C_LK).astype(BF16)
    kt = lax.dot_general(wkt_ref[...], xb, NT_DIMS, preferred_element_type=F32)
    lkt_ref[...] = (kt * QK_SCALE).astype(BF16)
    lv_ref[...] = seg(C_LV, C_LO).astype(BF16)
    lo_ref[...] = seg(C_LO, C_LG)
    g = lax.dot_general(wgt_ref[...], xb, NT_DIMS, preferred_element_type=F32) + gb_ref[...]
    row = lax.broadcasted_iota(jnp.int32, g.shape, 0)
    is_forget = ((row >> 2) & 1) == 1
    log_sig = jnp.minimum(g, 0.0) - jnp.log1p(jnp.exp(-jnp.abs(g)))
    lgt_ref[...] = jnp.where(is_forget, log_sig, g)


def _inproj(x2d, w_main, w_kt, w_gt, gbias, tm):
    t = x2d.shape[0]
    grid = (t // tm,)
    row_blk = lambda n: pl.BlockSpec((tm, n), lambda i: (i, 0))
    full = lambda a: pl.BlockSpec(a.shape, lambda i: (0, 0))
    out_shapes = (
        jax.ShapeDtypeStruct((t, 512), BF16),
        jax.ShapeDtypeStruct((t, 512), BF16),
        jax.ShapeDtypeStruct((t, ATT_WIDTH), BF16),
        jax.ShapeDtypeStruct((t, 256), BF16),
        jax.ShapeDtypeStruct((256, t), BF16),
        jax.ShapeDtypeStruct((t, LSTM_WIDTH), BF16),
        jax.ShapeDtypeStruct((t, LSTM_WIDTH), F32),
        jax.ShapeDtypeStruct((N_GATES, t), F32),
    )
    out_specs = (
        row_blk(512), row_blk(512), row_blk(ATT_WIDTH), row_blk(256),
        pl.BlockSpec((256, tm), lambda i: (0, i)),
        row_blk(LSTM_WIDTH), row_blk(LSTM_WIDTH),
        pl.BlockSpec((N_GATES, tm), lambda i: (0, i)),
    )
    return pl.pallas_call(
        _inproj_kernel,
        grid=grid,
        in_specs=[row_blk(D_MODEL), full(w_main), full(w_kt), full(w_gt), full(gbias)],
        out_specs=out_specs,
        out_shape=out_shapes,
        compiler_params=_params(("parallel",)),
        name="inproj",
    )(x2d, w_main, w_kt, w_gt, gbias)


def _alibi_slope(h):
    return jnp.where(h == 0, 0.25, jnp.where(h == 1, 0.0625, jnp.where(h == 2, 0.015625, 0.00390625))).astype(F32)


def _attn_kernel(q_ref, k_ref, v_ref, lq1_ref, lk1_ref, lq2_ref, lk2_ref, g_ref, o_ref,
                 kcat_ref, vaug_ref, qv_ref, s_ref, p_ref, *, seq, lam_init):
    n_tiles = seq // K_TILE
    h = pl.program_id(1)
    qi = pl.program_id(2)
    slope = _alibi_slope(h)

    @pl.when(qi == 0)
    def _build_keys():
        k = k_ref[...]
        lane = lax.broadcasted_iota(jnp.int32, (seq, LANES), 1)
        j = lax.broadcasted_iota(jnp.int32, (seq, LANES), 0)
        zero = jnp.zeros_like(k)
        k1 = jnp.where(lane < QK_DIM, k, zero)
        k2 = jnp.where(lane >= QK_DIM, k, zero)
        kpos = jnp.where(lane < 2, 1.0,
                         jnp.where(lane == 2, (j >> 6).astype(F32),
                                   jnp.where(lane == 3, (j & 63).astype(F32), 0.0))).astype(BF16)
        for t in range(n_tiles):
            rows = slice(t * K_TILE, (t + 1) * K_TILE)
            kcat_ref[t, 0:K_TILE, 0:LANES] = k1[rows]
            kcat_ref[t, 0:K_TILE, LANES:2 * LANES] = kpos[rows]
            kcat_ref[t, K_TILE:2 * K_TILE, 0:LANES] = k2[rows]
            kcat_ref[t, K_TILE:2 * K_TILE, LANES:2 * LANES] = kpos[rows]
        vaug_ref[:, 0:V_DIM] = v_ref[...]
        vaug_ref[:, V_DIM:2 * V_DIM] = jnp.ones((seq, V_DIM), BF16)

    q = q_ref[...]
    lane = lax.broadcasted_iota(jnp.int32, (Q_BLOCK, LANES), 1)
    i = qi * Q_BLOCK + lax.broadcasted_iota(jnp.int32, (Q_BLOCK, LANES), 0)
    feat = slope * jnp.where(lane == 0, -64.0 * (i >> 6).astype(F32),
                             jnp.where(lane == 1, -(i & 63).astype(F32),
                                       jnp.where(lane == 2, 64.0, jnp.where(lane == 3, 1.0, 0.0))))
    for var in range(3):
        qv_ref[var, :, 0:LANES] = q
    qv_ref[0, :, LANES:2 * LANES] = feat.astype(BF16)
    qv_ref[1, :, LANES:2 * LANES] = (-feat).astype(BF16)
    qv_ref[2, :, LANES:2 * LANES] = jnp.zeros((Q_BLOCK, LANES), BF16)

    for t in range(n_tiles):
        sel = jnp.where(t < qi, 0, jnp.where(t == qi, 2, 1))
        s = lax.dot_general(qv_ref[sel], kcat_ref[t], NT_DIMS, preferred_element_type=F32)
        s_ref[0, t] = s[:, 0:K_TILE]
        s_ref[1, t] = s[:, K_TILE:2 * K_TILE]
    r = lax.broadcasted_iota(jnp.int32, (Q_BLOCK, K_TILE), 0)
    c = lax.broadcasted_iota(jnp.int32, (Q_BLOCK, K_TILE), 1)
    diag_bias = -slope * jnp.abs(r - c).astype(F32)
    s_ref[0, qi] = s_ref[0, qi] + diag_bias
    s_ref[1, qi] = s_ref[1, qi] + diag_bias

    outs = []
    for m in range(2):
        mx = s_ref[m, 0]
        for t in range(1, n_tiles):
            mx = jnp.maximum(mx, s_ref[m, t])
        row_max = jnp.broadcast_to(jnp.max(mx, axis=1, keepdims=True), (Q_BLOCK, K_TILE))
        for t in range(n_tiles):
            p_ref[m, t] = jnp.exp(s_ref[m, t] - row_max).astype(BF16)
        acc = jnp.zeros((Q_BLOCK, 2 * V_DIM), F32)
        for t in range(n_tiles):
            acc = acc + jnp.dot(p_ref[m, t], vaug_ref[t * K_TILE:(t + 1) * K_TILE, :],
                                preferred_element_type=F32)
        outs.append(acc[:, 0:V_DIM] / acc[:, V_DIM:2 * V_DIM])

    lam = (jnp.exp(jnp.sum(lq1_ref[...] * lk1_ref[...], keepdims=True))
           - jnp.exp(jnp.sum(lq2_ref[...] * lk2_ref[...], keepdims=True)) + lam_init)
    att = outs[0] - lam * outs[1]
    att = att * lax.rsqrt(jnp.mean(att * att, axis=-1, keepdims=True) + EPS)
    o_ref[...] = (att * g_ref[...] * (1.0 - lam_init)).astype(o_ref.dtype)


def _attention(aq, ak, av, lq1, lk1, lq2, lk2, att_g, batch, seq, lam_init):
    t = batch * seq
    nq = seq // Q_BLOCK
    n_tiles = seq // K_TILE
    grid = (batch, HEADS, nq)
    small = pl.BlockSpec((1, QK_DIM), lambda b, h, q: (0, 0))
    kernel = functools.partial(_attn_kernel, seq=seq, lam_init=lam_init)
    return pl.pallas_call(
        kernel,
        grid=grid,
        in_specs=[
            pl.BlockSpec((Q_BLOCK, LANES), lambda b, h, q: (b * nq + q, h)),
            pl.BlockSpec((seq, LANES), lambda b, h, q: (b, h)),
            pl.BlockSpec((seq, V_DIM), lambda b, h, q: (b, h)),
            small, small, small, small,
            pl.BlockSpec((1, V_DIM), lambda b, h, q: (0, h)),
        ],
        out_specs=pl.BlockSpec((Q_BLOCK, V_DIM), lambda b, h, q: (b * nq + q, h)),
        out_shape=jax.ShapeDtypeStruct((t, ATT_WIDTH), BF16),
        scratch_shapes=[
            pltpu.VMEM((n_tiles, 2 * K_TILE, 2 * LANES), BF16),
            pltpu.VMEM((seq, 2 * V_DIM), BF16),
            pltpu.VMEM((3, Q_BLOCK, 2 * LANES), BF16),
            pltpu.VMEM((2, n_tiles, Q_BLOCK, K_TILE), F32),
            pltpu.VMEM((2, n_tiles, Q_BLOCK, K_TILE), BF16),
        ],
        compiler_params=_params(("parallel", "parallel", "arbitrary")),
        name="diff_attention",
    )(aq, ak, av, lq1, lk1, lq2, lk2, att_g)


def _split3(x):
    hi = x.astype(BF16).astype(F32)
    r1 = x - hi
    mid = r1.astype(BF16).astype(F32)
    lo = (r1 - mid).astype(BF16).astype(F32)
    return hi, mid, lo


def _chunk_scan(x, pos, combine, fill, reverse, seq):
    k = 1
    while k < CHUNK:
        if reverse:
            shifted = pltpu.roll(x, seq - k, 1)
            ok = pos < CHUNK - k
        else:
            shifted = pltpu.roll(x, k, 1)
            ok = pos >= k
        x = combine(x, jnp.where(ok, shifted, fill))
        k *= 2
    return x


def _mlstm_kernel(q_ref, kt_ref, v_ref, o_ref_in, gt_ref, g_ref, out_ref,
                  rows_ref, hacc_ref, kz_ref, c_ref, cext_ref, lhs_ref, rhs_ref, *, seq):
    L = CHUNK
    nc = seq // L
    h = pl.program_id(1)

    gts = gt_ref[...]
    row = lax.broadcasted_iota(jnp.int32, gts.shape, 0)
    pos = lax.broadcasted_iota(jnp.int32, gts.shape, 1) & (L - 1)
    psum = _chunk_scan(gts, pos, jnp.add, 0.0, False, seq)
    ssum = _chunk_scan(gts, pos, jnp.add, 0.0, True, seq)
    bsum = jnp.where(row < 2 * HEADS, psum, ssum)
    u = gts - pltpu.roll(bsum, N_GATES - HEADS, 0)
    cmax = jnp.where(row < 2 * HEADS,
                     _chunk_scan(u, pos, jnp.maximum, -jnp.inf, False, seq),
                     _chunk_scan(u, pos, jnp.maximum, -jnp.inf, True, seq))
    def pick(x, r):
        return jnp.sum(jnp.where(row == r, x, 0.0), axis=0, keepdims=True)

    for d in range(2):
        rows_ref[3 * d:3 * d + 1, :] = pick(u, 8 * d + h)
        rows_ref[3 * d + 1:3 * d + 2, :] = pick(bsum, 8 * d + HEADS + h)
        rows_ref[3 * d + 2:3 * d + 3, :] = pick(cmax, 8 * d + h)

    hacc_ref[...] = jnp.zeros_like(hacc_ref)
    kz_ref[...] = jnp.zeros_like(kz_ref)
    c_ref[...] = jnp.zeros_like(c_ref)
    cext_ref[...] = jnp.zeros_like(cext_ref)
    lrow = lax.broadcasted_iota(jnp.int32, (16, L), 0)
    lhs_const = jnp.where(lrow < 3, 1.0, 0.0).astype(F32)
    rrow = lax.broadcasted_iota(jnp.int32, (16, 4 * L), 0)
    rcol = lax.broadcasted_iota(jnp.int32, (16, 4 * L), 1) >> 7
    in35 = (rrow >= 3) & (rrow < 6)
    rhs_const = jnp.where((in35 & (rcol < 3)) | ((rrow >= 9) & (rrow < 12) & (rcol == 2))
                          | ((rrow >= 6) & (rrow < 9) & (rcol == 3)), 1.0, 0.0).astype(F32)
    for d in range(2):
        lhs_ref[d] = lhs_const
        rhs_ref[d] = rhs_const

    jj = lax.broadcasted_iota(jnp.int32, (L, L), 0)
    ss = lax.broadcasted_iota(jnp.int32, (L, L), 1)
    masks = (ss <= jj, ss >= jj)
    ones_v = jnp.ones((L, V_DIM), BF16)
    hrow = pl.multiple_of(h * QK_DIM, QK_DIM)

    def chunk_step(d, c, m_prev):
        c0 = pl.multiple_of(c * L, L)
        cols = pl.ds(c0, L)
        u_row = rows_ref[3 * d:3 * d + 1, cols]
        b_row = rows_ref[3 * d + 1:3 * d + 2, cols]
        cm_row = rows_ref[3 * d + 2:3 * d + 3, cols]
        umax = jnp.max(u_row, axis=1, keepdims=True)
        g_tot = b_row[:, L - 1:L] if d == 0 else b_row[:, 0:1]
        mx_row = jnp.maximum(m_prev, cm_row)

        for r0, piece in ((3, -mx_row), (6, u_row), (9, -b_row)):
            for n, part in enumerate(_split3(piece)):
                lhs_ref[d, r0 + n:r0 + n + 1, :] = part
        for n, part in enumerate(_split3(u_row)):
            rhs_ref[d, n:n + 1, 0:L] = part
        for n, part in enumerate(_split3(m_prev)):
            rhs_ref[d, n:n + 1, L:2 * L] = jnp.broadcast_to(part, (1, L))
        for n, part in enumerate(_split3(-umax)):
            rhs_ref[d, n:n + 1, 3 * L:4 * L] = jnp.broadcast_to(part, (1, L))
        outer = lax.dot_general(lhs_ref[d].astype(BF16), rhs_ref[d].astype(BF16), TN_DIMS,
                                preferred_element_type=F32)
        e = jnp.exp(outer)
        p = jnp.where(masks[d], e[:, 0:L], 0.0)
        e_inter = e[:, L:2 * L]
        clamp = e[:, 2 * L:3 * L]
        e_end = e[:, 3 * L:4 * L]

        q = q_ref[cols, :]
        kh = kt_ref[pl.ds(hrow, QK_DIM), cols]
        kz_ref[d, pl.ds(hrow, QK_DIM), :] = kh
        qk = jnp.dot(q, kz_ref[d], preferred_element_type=F32)
        sqk = (qk * p).astype(BF16)
        vaug = jnp.concatenate([v_ref[cols, :], ones_v], axis=1)
        intra = jnp.dot(sqk, vaug, preferred_element_type=F32)
        inter = jnp.dot(q, cext_ref[d], preferred_element_type=F32)
        nd = intra + jnp.concatenate([e_inter, e_inter], axis=1) * inter
        hout = nd[:, 0:V_DIM] / jnp.maximum(jnp.abs(nd[:, V_DIM:2 * V_DIM]), clamp)
        hacc_ref[cols, :] = hacc_ref[cols, :] + hout

        ve = (vaug.astype(F32) * jnp.concatenate([e_end, e_end], axis=1)).astype(BF16)
        c_loc = jnp.dot(kh, ve, preferred_element_type=F32)
        m_loc = g_tot + umax
        m_new = jnp.maximum(g_tot + m_prev, m_loc)
        a = jnp.exp(g_tot + m_prev - m_new)
        bb = jnp.exp(m_loc - m_new)
        c_new = a * c_ref[d] + bb * c_loc
        c_ref[d] = c_new
        cext_ref[d, pl.ds(hrow, QK_DIM), :] = c_new.astype(BF16)
        return m_new

    def body(it, carry):
        m_f, m_b = carry
        m_f = chunk_step(0, it, m_f)
        m_b = chunk_step(1, nc - 1 - it, m_b)
        return m_f, m_b

    zero = jnp.zeros((1, 1), F32)
    lax.fori_loop(0, nc, body, (zero, zero))

    hs = hacc_ref[...]
    hn = hs * lax.rsqrt(jnp.mean(hs * hs, axis=-1, keepdims=True) + EPS) * g_ref[...]
    out_ref[...] = (jax.nn.sigmoid(o_ref_in[...]) * hn).astype(out_ref.dtype)


def _mlstm(lq, lkt, lv, lo, lgt, lstm_g, batch, seq):
    t = batch * seq
    kernel = functools.partial(_mlstm_kernel, seq=seq)
    L = CHUNK
    return pl.pallas_call(
        kernel,
        grid=(batch, HEADS),
        in_specs=[
            pl.BlockSpec((seq, 256), lambda b, h: (b, 0)),
            pl.BlockSpec((256, seq), lambda b, h: (0, b)),
            pl.BlockSpec((seq, V_DIM), lambda b, h: (b, h)),
            pl.BlockSpec((seq, V_DIM), lambda b, h: (b, h)),
            pl.BlockSpec((N_GATES, seq), lambda b, h: (0, b)),
            pl.BlockSpec((1, V_DIM), lambda b, h: (0, h)),
        ],
        out_specs=pl.BlockSpec((seq, V_DIM), lambda b, h: (b, h)),
        out_shape=jax.ShapeDtypeStruct((t, LSTM_WIDTH), BF16),
        scratch_shapes=[
            pltpu.VMEM((8, seq), F32),
            pltpu.VMEM((seq, V_DIM), F32),
            pltpu.VMEM((2, 256, L), BF16),
            pltpu.VMEM((2, QK_DIM, 2 * V_DIM), F32),
            pltpu.VMEM((2, 256, 2 * V_DIM), BF16),
            pltpu.VMEM((2, 16, L), F32),
            pltpu.VMEM((2, 16, 4 * L), F32),
        ],
        compiler_params=_params(("parallel", "arbitrary")),
        name="mlstm",
    )(lq, lkt, lv, lo, lgt, lstm_g)


def _outproj_kernel(att_ref, lstm_ref, x_ref, w_ref, g_ref, b_ref, o_ref):
    y = jnp.dot(att_ref[...], w_ref[0:ATT_WIDTH, :], preferred_element_type=F32)
    y = y + jnp.dot(lstm_ref[...], w_ref[ATT_WIDTH:ATT_WIDTH + LSTM_WIDTH, :], preferred_element_type=F32)
    o_ref[...] = _layer_norm(ALPHA * x_ref[...] + y, g_ref[...], b_ref[...])


def _outproj(att, lstm, x2d, w_out, g, b, tm):
    t = x2d.shape[0]
    row_blk = lambda n: pl.BlockSpec((tm, n), lambda i: (i, 0))
    full = lambda a: pl.BlockSpec(a.shape, lambda i: (0, 0))
    return pl.pallas_call(
        _outproj_kernel,
        grid=(t // tm,),
        in_specs=[row_blk(ATT_WIDTH), row_blk(LSTM_WIDTH), row_blk(D_MODEL), full(w_out), full(g), full(b)],
        out_specs=row_blk(D_MODEL),
        out_shape=jax.ShapeDtypeStruct((t, D_MODEL), F32),
        compiler_params=_params(("parallel",)),
        name="outproj_ln1",
    )(att, lstm, x2d, w_out, g, b)


HALO = 8


def _ffn_kernel(x_ref, xp_ref, xn_ref, wg_ref, wu_ref, cw_ref, cb_ref, wd_ref, g_ref, b_ref, o_ref,
                gs_ref, acc_ref, *, seq, tm):
    i = pl.program_id(0)
    kf = pl.program_id(1)
    nk = pl.num_programs(1)

    @pl.when(kf == 0)
    def _init():
        acc_ref[...] = jnp.zeros_like(acc_ref)

    xb = x_ref[...].astype(BF16)
    has_prev = ((i * tm) % seq != 0).astype(F32)
    has_next = (((i + 1) * tm) % seq != 0).astype(F32)
    halo = jnp.concatenate([xp_ref[...] * has_prev, xn_ref[...] * has_next], axis=0).astype(BF16)
    gs_ref[HALO:HALO + tm, :] = jnp.dot(xb, wg_ref[...], preferred_element_type=F32)
    gh = jnp.dot(halo, wg_ref[...], preferred_element_type=F32)
    gs_ref[0:HALO, :] = gh[0:HALO]
    gs_ref[HALO + tm:2 * HALO + tm, :] = gh[HALO:2 * HALO]
    conv = (cb_ref[...]
            + gs_ref[HALO - 1:HALO - 1 + tm, :] * cw_ref[0:1, :]
            + gs_ref[HALO:HALO + tm, :] * cw_ref[1:2, :]
            + gs_ref[HALO + 1:HALO + 1 + tm, :] * cw_ref[2:3, :])
    up = jnp.dot(xb, wu_ref[...], preferred_element_type=F32)
    gelu = 0.5 * conv * (lax.erf(conv * (1.0 / math.sqrt(2.0))) + 1.0)
    hmid = (gelu * up).astype(BF16)
    acc_ref[...] += jnp.dot(hmid, wd_ref[...], preferred_element_type=F32)

    @pl.when(kf == nk - 1)
    def _fin():
        o_ref[...] = _layer_norm(ALPHA * x_ref[...] + acc_ref[...], g_ref[...], b_ref[...])


def _ffn(x2d, w_g, w_u, conv_w, conv_b, w_d, g, b, seq, tm):
    t = x2d.shape[0]
    nk = D_FF // FF_BLOCK
    hb = tm // HALO
    last_hblk = t // HALO - 1
    kernel = functools.partial(_ffn_kernel, seq=seq, tm=tm)
    return pl.pallas_call(
        kernel,
        grid=(t // tm, nk),
        in_specs=[
            pl.BlockSpec((tm, D_MODEL), lambda i, k: (i, 0)),
            pl.BlockSpec((HALO, D_MODEL), lambda i, k: (jnp.maximum(i * hb - 1, 0), 0)),
            pl.BlockSpec((HALO, D_MODEL), lambda i, k: (jnp.minimum((i + 1) * hb, last_hblk), 0)),
            pl.BlockSpec((D_MODEL, FF_BLOCK), lambda i, k: (0, k)),
            pl.BlockSpec((D_MODEL, FF_BLOCK), lambda i, k: (0, k)),
            pl.BlockSpec((3, FF_BLOCK), lambda i, k: (0, k)),
            pl.BlockSpec((1, FF_BLOCK), lambda i, k: (0, k)),
            pl.BlockSpec((FF_BLOCK, D_MODEL), lambda i, k: (k, 0)),
            pl.BlockSpec((1, D_MODEL), lambda i, k: (0, 0)),
            pl.BlockSpec((1, D_MODEL), lambda i, k: (0, 0)),
        ],
        out_specs=pl.BlockSpec((tm, D_MODEL), lambda i, k: (i, 0)),
        out_shape=jax.ShapeDtypeStruct((t, D_MODEL), F32),
        scratch_shapes=[
            pltpu.VMEM((tm + 2 * HALO, FF_BLOCK), F32),
            pltpu.VMEM((tm, D_MODEL), F32),
        ],
        compiler_params=_params(("parallel", "arbitrary")),
        name="conv_ffn_ln2",
    )(x2d, x2d, x2d, w_g, w_u, conv_w, conv_b, w_d, g, b)


def _prep_layer(l, w_in, gate_bias, lam_q1, lam_k1, lam_q2, lam_k2, att_norm_g, lstm_norm_g, w_out,
                ln1_g, ln1_b, w_gu, conv_w, conv_b, w_down, ln2_g, ln2_b):
    wl = w_in[l]
    row = lambda a: a.reshape(1, -1).astype(F32)
    return dict(
        w_main=wl[:, :C_LG].astype(BF16),
        w_kt=wl[:, C_LK:C_LV].T.astype(BF16),
        w_gt=wl[:, C_LG:C_END].T.astype(BF16),
        gbias=gate_bias[l].reshape(N_GATES, 1).astype(F32),
        lq1=row(lam_q1[l]), lk1=row(lam_k1[l]), lq2=row(lam_q2[l]), lk2=row(lam_k2[l]),
        att_g=row(att_norm_g[l]), lstm_g=row(lstm_norm_g[l]),
        w_out=w_out[l].astype(BF16), ln1_g=row(ln1_g[l]), ln1_b=row(ln1_b[l]),
        w_g=w_gu[l][:, :D_FF].astype(BF16), w_u=w_gu[l][:, D_FF:].astype(BF16),
        conv_w=conv_w[l].astype(F32), conv_b=row(conv_b[l]),
        w_d=w_down[l].astype(BF16), ln2_g=row(ln2_g[l]), ln2_b=row(ln2_b[l]),
    )


def _trunk(x, layers, tm=TOKEN_BLOCK):
    batch, seq, _ = x.shape
    x2d = x.reshape(batch * seq, D_MODEL)
    for l, p in enumerate(layers):
        lam_init = 0.8 - 0.6 * math.exp(-0.3 * l)
        aq, ak, av, lq, lkt, lv, lo, lgt = _inproj(x2d, p["w_main"], p["w_kt"], p["w_gt"], p["gbias"], tm)
        att = _attention(aq, ak, av, p["lq1"], p["lk1"], p["lq2"], p["lk2"], p["att_g"], batch, seq, lam_init)
        lstm = _mlstm(lq, lkt, lv, lo, lgt, p["lstm_g"], batch, seq)
        x2d = _outproj(att, lstm, x2d, p["w_out"], p["ln1_g"], p["ln1_b"], tm)
        x2d = _ffn(x2d, p["w_g"], p["w_u"], p["conv_w"], p["conv_b"], p["w_d"], p["ln2_g"], p["ln2_b"], seq, tm)
    return x2d.reshape(batch, seq, D_MODEL)


def kernel(x_prompt, x_sample, w_in, gate_bias, lam_q1, lam_k1, lam_q2, lam_k2, att_norm_g, lstm_norm_g,
           w_out, ln1_g, ln1_b, w_gu, conv_w, conv_b, w_down, ln2_g, ln2_b):
    layers = [_prep_layer(l, w_in, gate_bias, lam_q1, lam_k1, lam_q2, lam_k2, att_norm_g, lstm_norm_g,
                          w_out, ln1_g, ln1_b, w_gu, conv_w, conv_b, w_down, ln2_g, ln2_b)
              for l in range(DEPTH)]
    return (_trunk(x_prompt, layers), _trunk(x_sample, layers))
```

```python
import functools
import math

import jax
import jax.numpy as jnp
from jax import lax
from jax.experimental import pallas as pl
from jax.experimental.pallas import tpu as pltpu

F32 = jnp.float32
BF16 = jnp.bfloat16

D_MODEL = 1024
DEPTH = 2
HEADS = 4
QK_DIM = 64
V_DIM = 128
ATT_WIDTH = HEADS * V_DIM
LSTM_WIDTH = HEADS * V_DIM
N_GATES = 4 * HEADS
D_FF = 2816
CHUNK = 128
ALPHA = (2 * DEPTH) ** 0.25
EPS = 1e-5
QK_SCALE = QK_DIM ** -0.5

C_AQ, C_AK, C_AV, C_LQ, C_LK, C_LV, C_LO, C_LG, C_END = 0, 512, 1024, 1536, 1792, 2048, 2560, 3072, 3088

LANES = 128
MXU_TILE = 256
VMEM_LIMIT = 56 * 1024 * 1024

TOKEN_BLOCK = 512
FF_BOUNDS = (0, 6 * MXU_TILE, D_FF)
Q_BLOCK = MXU_TILE
K_TILE = MXU_TILE

NT_DIMS = (((1,), (1,)), ((), ()))
TN_DIMS = (((0,), (0,)), ((), ()))


def _params(sem):
    return pltpu.CompilerParams(dimension_semantics=sem, vmem_limit_bytes=VMEM_LIMIT)


def _layer_norm(z, g, b):
    mu = jnp.mean(z, axis=-1, keepdims=True)
    zc = z - mu
    var = jnp.mean(zc * zc, axis=-1, keepdims=True)
    return zc * lax.rsqrt(var + EPS) * g + b


def _inproj_kernel(x_ref, w_ref, wkt_ref, wgt_ref, gb_ref,
                   aq_ref, ak_ref, av_ref, lq_ref, lkt_ref, lv_ref, lo_ref, lgt_ref):
    xb = x_ref[...].astype(BF16)

    def seg(a, b):
        return jnp.dot(xb, w_ref[:, a:b], preferred_element_type=F32)

    aq_ref[...] = (seg(C_AQ, C_AK) * QK_SCALE).astype(BF16)
    ak_ref[...] = seg(C_AK, C_AV).astype(BF16)
    av_ref[...] = seg(C_AV, C_LQ).astype(BF16)
    lq_ref[...] = seg(C_LQ, C_LK).astype(BF16)
    kt = lax.dot_general(wkt_ref[...], xb, NT_DIMS, preferred_element_type=F32)
    lkt_ref[...] = (kt * QK_SCALE).astype(BF16)
    lv_ref[...] = seg(C_LV, C_LO).astype(BF16)
    lo_ref[...] = seg(C_LO, C_LG).astype(BF16)
    g = lax.dot_general(wgt_ref[...], xb, NT_DIMS, preferred_element_type=F32) + gb_ref[...]
    row = lax.broadcasted_iota(jnp.int32, g.shape, 0)
    is_forget = ((row >> 2) & 1) == 1
    log_sig = jnp.minimum(g, 0.0) - jnp.log1p(jnp.exp(-jnp.abs(g)))
    lgt_ref[...] = jnp.where(is_forget, log_sig, g)


def _inproj(x2d, w_main, w_kt, w_gt, gbias, tm):
    t = x2d.shape[0]
    grid = (t // tm,)
    row_blk = lambda n: pl.BlockSpec((tm, n), lambda i: (i, 0))
    full = lambda a: pl.BlockSpec(a.shape, lambda i: (0, 0))
    out_shapes = (
        jax.ShapeDtypeStruct((t, 512), BF16),
        jax.ShapeDtypeStruct((t, 512), BF16),
        jax.ShapeDtypeStruct((t, ATT_WIDTH), BF16),
        jax.ShapeDtypeStruct((t, 256), BF16),
        jax.ShapeDtypeStruct((256, t), BF16),
        jax.ShapeDtypeStruct((t, LSTM_WIDTH), BF16),
        jax.ShapeDtypeStruct((t, LSTM_WIDTH), BF16),
        jax.ShapeDtypeStruct((N_GATES, t), F32),
    )
    out_specs = (
        row_blk(512), row_blk(512), row_blk(ATT_WIDTH), row_blk(256),
        pl.BlockSpec((256, tm), lambda i: (0, i)),
        row_blk(LSTM_WIDTH), row_blk(LSTM_WIDTH),
        pl.BlockSpec((N_GATES, tm), lambda i: (0, i)),
    )
    return pl.pallas_call(
        _inproj_kernel,
        grid=grid,
        in_specs=[row_blk(D_MODEL), full(w_main), full(w_kt), full(w_gt), full(gbias)],
        out_specs=out_specs,
        out_shape=out_shapes,
        compiler_params=_params(("parallel",)),
        name="inproj",
    )(x2d, w_main, w_kt, w_gt, gbias)


def _alibi_slope(h):
    return jnp.where(h == 0, 0.25, jnp.where(h == 1, 0.0625, jnp.where(h == 2, 0.015625, 0.00390625))).astype(F32)


def _attn_kernel(q_ref, k_ref, v_ref, lq1_ref, lk1_ref, lq2_ref, lk2_ref, g_ref, o_ref,
                 kcat_ref, vaug_ref, qv_ref, sa_ref, sb_ref, p_ref, *, seq, n_blocks, lam_init):
    n_tiles = seq // K_TILE
    nq = seq // Q_BLOCK
    step = pl.program_id(0)
    blk = jnp.minimum(step, n_blocks - 1)
    head_idx = blk // nq
    h = head_idx % HEADS
    qi = blk % nq
    prev_head_idx = jnp.maximum(step - 1, 0) // nq
    slope = _alibi_slope(h)

    @pl.when(step == 0)
    def _zero_scores():
        sb_ref[...] = jnp.zeros_like(sb_ref)

    @pl.when((qi == 0) & (step < n_blocks))
    def _build_keys():
        k = k_ref[...]
        lane = lax.broadcasted_iota(jnp.int32, (seq, LANES), 1)
        j = lax.broadcasted_iota(jnp.int32, (seq, LANES), 0)
        zero = jnp.zeros_like(k)
        k1 = jnp.where(lane < QK_DIM, k, zero)
        k2 = jnp.where(lane >= QK_DIM, k, zero)
        kpos = jnp.where(lane < 2, 1.0,
                         jnp.where(lane == 2, (j >> 6).astype(F32),
                                   jnp.where(lane == 3, (j & 63).astype(F32), 0.0))).astype(BF16)
        for t in range(n_tiles):
            rows = slice(t * K_TILE, (t + 1) * K_TILE)
            kcat_ref[t, 0:K_TILE, 0:LANES] = k1[rows]
            kcat_ref[t, 0:K_TILE, LANES:2 * LANES] = kpos[rows]
            kcat_ref[t, K_TILE:2 * K_TILE, 0:LANES] = k2[rows]
            kcat_ref[t, K_TILE:2 * K_TILE, LANES:2 * LANES] = kpos[rows]
        slot = head_idx % 2
        vaug_ref[slot, :, 0:V_DIM] = v_ref[...]
        vaug_ref[slot, :, V_DIM:2 * V_DIM] = jnp.ones((seq, V_DIM), BF16)

    def block_step(s_new, s_old):
        q = q_ref[...]
        lane = lax.broadcasted_iota(jnp.int32, (Q_BLOCK, LANES), 1)
        i = qi * Q_BLOCK + lax.broadcasted_iota(jnp.int32, (Q_BLOCK, LANES), 0)
        feat = slope * jnp.where(lane == 0, -64.0 * (i >> 6).astype(F32),
                                 jnp.where(lane == 1, -(i & 63).astype(F32),
                                           jnp.where(lane == 2, 64.0, jnp.where(lane == 3, 1.0, 0.0))))
        for var in range(3):
            qv_ref[var, :, 0:LANES] = q
        qv_ref[0, :, LANES:2 * LANES] = feat.astype(BF16)
        qv_ref[1, :, LANES:2 * LANES] = (-feat).astype(BF16)
        qv_ref[2, :, LANES:2 * LANES] = jnp.zeros((Q_BLOCK, LANES), BF16)
        for t in range(n_tiles):
            sel = jnp.where(t < qi, 0, jnp.where(t == qi, 2, 1))
            s = lax.dot_general(qv_ref[sel], kcat_ref[t], NT_DIMS, preferred_element_type=F32)
            s_new[0, t] = s[:, 0:K_TILE]
            s_new[1, t] = s[:, K_TILE:2 * K_TILE]
        r = lax.broadcasted_iota(jnp.int32, (Q_BLOCK, K_TILE), 0)
        c = lax.broadcasted_iota(jnp.int32, (Q_BLOCK, K_TILE), 1)
        diag_bias = -slope * jnp.abs(r - c).astype(F32)
        s_new[0, qi] = s_new[0, qi] + diag_bias
        s_new[1, qi] = s_new[1, qi] + diag_bias

        vslot = prev_head_idx % 2
        outs = []
        for m in range(2):
            mx = s_old[m, 0]
            for t in range(1, n_tiles):
                mx = jnp.maximum(mx, s_old[m, t])
            row_max = jnp.broadcast_to(jnp.max(mx, axis=1, keepdims=True), (Q_BLOCK, K_TILE))
            acc = jnp.zeros((Q_BLOCK, 2 * V_DIM), F32)
            for t in range(n_tiles):
                p_ref[m, t] = jnp.exp(s_old[m, t] - row_max).astype(BF16)
                acc = acc + jnp.dot(p_ref[m, t], vaug_ref[vslot, t * K_TILE:(t + 1) * K_TILE, :],
                                    preferred_element_type=F32)
            outs.append(acc[:, 0:V_DIM] / acc[:, V_DIM:2 * V_DIM])
        lam = (jnp.exp(jnp.sum(lq1_ref[...] * lk1_ref[...], keepdims=True))
               - jnp.exp(jnp.sum(lq2_ref[...] * lk2_ref[...], keepdims=True)) + lam_init)
        att = outs[0] - lam * outs[1]
        att = att * lax.rsqrt(jnp.mean(att * att, axis=-1, keepdims=True) + EPS)
        o_ref[...] = (att * g_ref[...] * (1.0 - lam_init)).astype(o_ref.dtype)

    @pl.when(step % 2 == 0)
    def _even():
        block_step(sa_ref, sb_ref)

    @pl.when(step % 2 == 1)
    def _odd():
        block_step(sb_ref, sa_ref)


def _attention(aq, ak, av, lq1, lk1, lq2, lk2, att_g, batch, seq, lam_init):
    t = batch * seq
    nq = seq // Q_BLOCK
    n_tiles = seq // K_TILE
    n_blocks = batch * HEADS * nq

    def cur(i):
        blk = jnp.minimum(i, n_blocks - 1)
        return blk // (HEADS * nq), (blk // nq) % HEADS, blk % nq

    def prev(i):
        blk = jnp.maximum(i - 1, 0)
        return blk // (HEADS * nq), (blk // nq) % HEADS, blk % nq

    def q_map(i):
        b, h, q = cur(i)
        return b * nq + q, h

    def kv_map(i):
        b, h, _ = cur(i)
        return b, h

    def out_map(i):
        b, h, q = prev(i)
        return b * nq + q, h

    small = pl.BlockSpec((1, QK_DIM), lambda i: (0, 0))
    kernel = functools.partial(_attn_kernel, seq=seq, n_blocks=n_blocks, lam_init=lam_init)
    score_buf = pltpu.VMEM((2, n_tiles, Q_BLOCK, K_TILE), F32)
    return pl.pallas_call(
        kernel,
        grid=(n_blocks + 1,),
        in_specs=[
            pl.BlockSpec((Q_BLOCK, LANES), q_map),
            pl.BlockSpec((seq, LANES), kv_map),
            pl.BlockSpec((seq, V_DIM), kv_map),
            small, small, small, small,
            pl.BlockSpec((1, V_DIM), lambda i: (0, prev(i)[1])),
        ],
        out_specs=pl.BlockSpec((Q_BLOCK, V_DIM), out_map),
        out_shape=jax.ShapeDtypeStruct((t, ATT_WIDTH), BF16),
        scratch_shapes=[
            pltpu.VMEM((n_tiles, 2 * K_TILE, 2 * LANES), BF16),
            pltpu.VMEM((2, seq, 2 * V_DIM), BF16),
            pltpu.VMEM((3, Q_BLOCK, 2 * LANES), BF16),
            score_buf, score_buf,
            pltpu.VMEM((2, n_tiles, Q_BLOCK, K_TILE), BF16),
        ],
        compiler_params=_params(("arbitrary",)),
        name="diff_attention",
    )(aq, ak, av, lq1, lk1, lq2, lk2, att_g)


def _split3(x):
    hi = x.astype(BF16).astype(F32)
    r1 = x - hi
    mid = r1.astype(BF16).astype(F32)
    lo = (r1 - mid).astype(BF16).astype(F32)
    return hi, mid, lo


def _chunk_scan(x, pos, combine, fill, reverse, seq):
    k = 1
    while k < CHUNK:
        if reverse:
            shifted = pltpu.roll(x, seq - k, 1)
            ok = pos < CHUNK - k
        else:
            shifted = pltpu.roll(x, k, 1)
            ok = pos >= k
        x = combine(x, jnp.where(ok, shifted, fill))
        k *= 2
    return x


def _mlstm_kernel(q_ref, kt_ref, v_ref, o_ref_in, gt_ref, g_ref, out_ref,
                  u_ref, bs_ref, cm_ref, hacc_ref, kz_ref, c_ref, cext_ref, lhs_ref, rhs_ref, *, seq):
    L = CHUNK
    nc = seq // L

    gts = gt_ref[...]
    row = lax.broadcasted_iota(jnp.int32, gts.shape, 0)
    pos = lax.broadcasted_iota(jnp.int32, gts.shape, 1) & (L - 1)
    psum = _chunk_scan(gts, pos, jnp.add, 0.0, False, seq)
    ssum = _chunk_scan(gts, pos, jnp.add, 0.0, True, seq)
    bsum = jnp.where(row < 2 * HEADS, psum, ssum)
    u = gts - pltpu.roll(bsum, N_GATES - HEADS, 0)
    cmax = jnp.where(row < 2 * HEADS,
                     _chunk_scan(u, pos, jnp.maximum, -jnp.inf, False, seq),
                     _chunk_scan(u, pos, jnp.maximum, -jnp.inf, True, seq))
    u_ref[...] = u
    bs_ref[...] = bsum
    cm_ref[...] = cmax

    hacc_ref[...] = jnp.zeros_like(hacc_ref)
    kz_ref[...] = jnp.zeros_like(kz_ref)
    c_ref[...] = jnp.zeros_like(c_ref)
    cext_ref[...] = jnp.zeros_like(cext_ref)
    lrow = lax.broadcasted_iota(jnp.int32, (16, L), 0)
    lhs_const = jnp.where(lrow < 3, 1.0, 0.0).astype(F32)
    rrow = lax.broadcasted_iota(jnp.int32, (16, 3 * L), 0)
    rcol = lax.broadcasted_iota(jnp.int32, (16, 3 * L), 1) >> 7
    rhs_const = jnp.where(((rrow >= 3) & (rrow < 6)) | ((rrow >= 6) & (rrow < 9) & (rcol == 2)),
                          1.0, 0.0).astype(F32)
    for n in range(2 * HEADS):
        lhs_ref[n] = lhs_const
        rhs_ref[n] = rhs_const

    jj = lax.broadcasted_iota(jnp.int32, (L, L), 0)
    ss = lax.broadcasted_iota(jnp.int32, (L, L), 1)
    masks = (ss <= jj, ss >= jj)
    ones_v = jnp.ones((L, V_DIM), BF16)

    def chunk_step(d, h, c, m_prev):
        n = d * HEADS + h
        c0 = pl.multiple_of(c * L, L)
        cols = pl.ds(c0, L)
        hrows = slice(h * QK_DIM, (h + 1) * QK_DIM)
        hcols = slice(h * V_DIM, (h + 1) * V_DIM)
        u_row = u_ref[8 * d + h:8 * d + h + 1, cols]
        b_row = bs_ref[8 * d + HEADS + h:8 * d + HEADS + h + 1, cols]
        cm_row = cm_ref[8 * d + h:8 * d + h + 1, cols]
        umax = jnp.max(u_row, axis=1, keepdims=True)
        g_tot = b_row[:, L - 1:L] if d == 0 else b_row[:, 0:1]
        mx_row = jnp.maximum(m_prev, cm_row)

        for r0, piece in ((3, -mx_row), (6, -b_row)):
            for k, part in enumerate(_split3(piece)):
                lhs_ref[n, r0 + k:r0 + k + 1, :] = part
        for k, part in enumerate(_split3(u_row)):
            rhs_ref[n, k:k + 1, 0:L] = part
        for k, part in enumerate(_split3(m_prev)):
            rhs_ref[n, k:k + 1, L:2 * L] = jnp.broadcast_to(part, (1, L))
        outer = lax.dot_general(lhs_ref[n].astype(BF16), rhs_ref[n].astype(BF16), TN_DIMS,
                                preferred_element_type=F32)
        e = jnp.exp(outer)
        p = jnp.where(masks[d], e[:, 0:L], 0.0)
        e_inter = e[:, L:2 * L]
        clamp = e[:, 2 * L:3 * L]

        q = q_ref[cols, :]
        kh = kt_ref[hrows, cols]
        kz_ref[n, hrows, :] = kh
        qk = jnp.dot(q, kz_ref[n], preferred_element_type=F32)
        sqk = (qk * p).astype(BF16)
        vaug = jnp.concatenate([v_ref[cols, hcols], ones_v], axis=1)
        intra = jnp.dot(sqk, vaug, preferred_element_type=F32)
        inter = jnp.dot(q, cext_ref[n], preferred_element_type=F32)
        nd = intra + jnp.concatenate([e_inter, e_inter], axis=1) * inter
        hout = nd[:, 0:V_DIM] / jnp.maximum(jnp.abs(nd[:, V_DIM:2 * V_DIM]), clamp)
        hacc_ref[cols, hcols] = hacc_ref[cols, hcols] + hout

        ke = (kh.astype(F32) * jnp.exp(u_row - umax)).astype(BF16)
        c_loc = jnp.dot(ke, vaug, preferred_element_type=F32)
        m_loc = g_tot + umax
        m_new = jnp.maximum(g_tot + m_prev, m_loc)
        a = jnp.exp(g_tot + m_prev - m_new)
        bb = jnp.exp(m_loc - m_new)
        c_new = a * c_ref[n] + bb * c_loc
        c_ref[n] = c_new
        cext_ref[n, hrows, :] = c_new.astype(BF16)
        return m_new

    def body(it, carry):
        new = []
        for d in range(2):
            c = it if d == 0 else nc - 1 - it
            for h in range(HEADS):
                new.append(chunk_step(d, h, c, carry[d * HEADS + h]))
        return tuple(new)

    zero = jnp.zeros((1, 1), F32)
    lax.fori_loop(0, nc, body, (zero,) * (2 * HEADS), unroll=2)

    for h in range(HEADS):
        hcols = slice(h * V_DIM, (h + 1) * V_DIM)
        hs = hacc_ref[:, hcols]
        hn = hs * lax.rsqrt(jnp.mean(hs * hs, axis=-1, keepdims=True) + EPS) * g_ref[:, hcols]
        out_ref[:, hcols] = (jax.nn.sigmoid(o_ref_in[:, hcols].astype(F32)) * hn).astype(out_ref.dtype)


def _mlstm(lq, lkt, lv, lo, lgt, lstm_g, batch, seq):
    t = batch * seq
    kernel = functools.partial(_mlstm_kernel, seq=seq)
    L = CHUNK
    n_rec = 2 * HEADS
    gate_rows = pltpu.VMEM((N_GATES, seq), F32)
    return pl.pallas_call(
        kernel,
        grid=(batch,),
        in_specs=[
            pl.BlockSpec((seq, 256), lambda b: (b, 0)),
            pl.BlockSpec((256, seq), lambda b: (0, b)),
            pl.BlockSpec((seq, LSTM_WIDTH), lambda b: (b, 0)),
            pl.BlockSpec((seq, LSTM_WIDTH), lambda b: (b, 0)),
            pl.BlockSpec((N_GATES, seq), lambda b: (0, b)),
            pl.BlockSpec((1, LSTM_WIDTH), lambda b: (0, 0)),
        ],
        out_specs=pl.BlockSpec((seq, LSTM_WIDTH), lambda b: (b, 0)),
        out_shape=jax.ShapeDtypeStruct((t, LSTM_WIDTH), BF16),
        scratch_shapes=[
            gate_rows,
            gate_rows,
            gate_rows,
            pltpu.VMEM((seq, LSTM_WIDTH), F32),
            pltpu.VMEM((n_rec, 256, L), BF16),
            pltpu.VMEM((n_rec, QK_DIM, 2 * V_DIM), F32),
            pltpu.VMEM((n_rec, 256, 2 * V_DIM), BF16),
            pltpu.VMEM((n_rec, 16, L), F32),
            pltpu.VMEM((n_rec, 16, 3 * L), F32),
        ],
        compiler_params=_params(("parallel",)),
        name="mlstm",
    )(lq, lkt, lv, lo, lgt, lstm_g)


def _outproj_kernel(att_ref, lstm_ref, x_ref, w_ref, g_ref, b_ref, o_ref):
    y = jnp.dot(att_ref[...], w_ref[0:ATT_WIDTH, :], preferred_element_type=F32)
    y = y + jnp.dot(lstm_ref[...], w_ref[ATT_WIDTH:ATT_WIDTH + LSTM_WIDTH, :], preferred_element_type=F32)
    o_ref[...] = _layer_norm(ALPHA * x_ref[...] + y, g_ref[...], b_ref[...])


def _outproj(att, lstm, x2d, w_out, g, b, tm):
    t = x2d.shape[0]
    row_blk = lambda n: pl.BlockSpec((tm, n), lambda i: (i, 0))
    full = lambda a: pl.BlockSpec(a.shape, lambda i: (0, 0))
    return pl.pallas_call(
        _outproj_kernel,
        grid=(t // tm,),
        in_specs=[row_blk(ATT_WIDTH), row_blk(LSTM_WIDTH), row_blk(D_MODEL), full(w_out), full(g), full(b)],
        out_specs=row_blk(D_MODEL),
        out_shape=jax.ShapeDtypeStruct((t, D_MODEL), F32),
        compiler_params=_params(("parallel",)),
        name="outproj_ln1",
    )(att, lstm, x2d, w_out, g, b)


HALO = 8


def _ffn_kernel(x_ref, xp_ref, xn_ref, wg_ref, wu_ref, cw_ref, cb_ref, wd_ref, g_ref, b_ref, o_ref,
                gs_ref, *, seq, tm):
    i = pl.program_id(0)
    xb = x_ref[...].astype(BF16)
    has_prev = ((i * tm) % seq != 0).astype(F32)
    has_next = (((i + 1) * tm) % seq != 0).astype(F32)
    halo = jnp.concatenate([xp_ref[...] * has_prev, xn_ref[...] * has_next], axis=0).astype(BF16)
    acc = None
    for j in range(len(FF_BOUNDS) - 1):
        cs = slice(FF_BOUNDS[j], FF_BOUNDS[j + 1])
        gs_ref[HALO:HALO + tm, cs] = jnp.dot(xb, wg_ref[:, cs], preferred_element_type=F32)
        gh = jnp.dot(halo, wg_ref[:, cs], preferred_element_type=F32)
        gs_ref[0:HALO, cs] = gh[0:HALO]
        gs_ref[HALO + tm:2 * HALO + tm, cs] = gh[HALO:2 * HALO]
        conv = (cb_ref[:, cs]
                + gs_ref[HALO - 1:HALO - 1 + tm, cs] * cw_ref[0:1, cs]
                + gs_ref[HALO:HALO + tm, cs] * cw_ref[1:2, cs]
                + gs_ref[HALO + 1:HALO + 1 + tm, cs] * cw_ref[2:3, cs])
        up = jnp.dot(xb, wu_ref[:, cs], preferred_element_type=F32)
        gelu = 0.5 * conv * (lax.erf(conv * (1.0 / math.sqrt(2.0))) + 1.0)
        hmid = (gelu * up).astype(BF16)
        part = jnp.dot(hmid, wd_ref[cs, :], preferred_element_type=F32)
        acc = part if acc is None else acc + part
    o_ref[...] = _layer_norm(ALPHA * x_ref[...] + acc, g_ref[...], b_ref[...])


def _ffn(x2d, w_g, w_u, conv_w, conv_b, w_d, g, b, seq, tm):
    t = x2d.shape[0]
    hb = tm // HALO
    last_hblk = t // HALO - 1
    kernel = functools.partial(_ffn_kernel, seq=seq, tm=tm)
    resident = lambda a: pl.BlockSpec(a.shape, lambda i: (0, 0), pipeline_mode=pl.Buffered(1))
    return pl.pallas_call(
        kernel,
        grid=(t // tm,),
        in_specs=[
            pl.BlockSpec((tm, D_MODEL), lambda i: (i, 0)),
            pl.BlockSpec((HALO, D_MODEL), lambda i: (jnp.maximum(i * hb - 1, 0), 0)),
            pl.BlockSpec((HALO, D_MODEL), lambda i: (jnp.minimum((i + 1) * hb, last_hblk), 0)),
            resident(w_g), resident(w_u), resident(conv_w), resident(conv_b), resident(w_d),
            resident(g), resident(b),
        ],
        out_specs=pl.BlockSpec((tm, D_MODEL), lambda i: (i, 0)),
        out_shape=jax.ShapeDtypeStruct((t, D_MODEL), F32),
        scratch_shapes=[
            pltpu.VMEM((tm + 2 * HALO, D_FF), F32),
        ],
        compiler_params=_params(("parallel",)),
        name="conv_ffn_ln2",
    )(x2d, x2d, x2d, w_g, w_u, conv_w, conv_b, w_d, g, b)


def _prep_layer(l, w_in, gate_bias, lam_q1, lam_k1, lam_q2, lam_k2, att_norm_g, lstm_norm_g, w_out,
                ln1_g, ln1_b, w_gu, conv_w, conv_b, w_down, ln2_g, ln2_b):
    wl = w_in[l]
    row = lambda a: a.reshape(1, -1).astype(F32)
    return dict(
        w_main=wl[:, :C_LG].astype(BF16),
        w_kt=wl[:, C_LK:C_LV].T.astype(BF16),
        w_gt=wl[:, C_LG:C_END].T.astype(BF16),
        gbias=gate_bias[l].reshape(N_GATES, 1).astype(F32),
        lq1=row(lam_q1[l]), lk1=row(lam_k1[l]), lq2=row(lam_q2[l]), lk2=row(lam_k2[l]),
        att_g=row(att_norm_g[l]), lstm_g=row(lstm_norm_g[l]),
        w_out=w_out[l].astype(BF16), ln1_g=row(ln1_g[l]), ln1_b=row(ln1_b[l]),
        w_g=w_gu[l][:, :D_FF].astype(BF16), w_u=w_gu[l][:, D_FF:].astype(BF16),
        conv_w=conv_w[l].astype(F32), conv_b=row(conv_b[l]),
        w_d=w_down[l].astype(BF16), ln2_g=row(ln2_g[l]), ln2_b=row(ln2_b[l]),
    )


def _trunk(x, layers, tm=TOKEN_BLOCK):
    batch, seq, _ = x.shape
    x2d = x.reshape(batch * seq, D_MODEL)
    for l, p in enumerate(layers):
        lam_init = 0.8 - 0.6 * math.exp(-0.3 * l)
        aq, ak, av, lq, lkt, lv, lo, lgt = _inproj(x2d, p["w_main"], p["w_kt"], p["w_gt"], p["gbias"], tm)
        att = _attention(aq, ak, av, p["lq1"], p["lk1"], p["lq2"], p["lk2"], p["att_g"], batch, seq, lam_init)
        lstm = _mlstm(lq, lkt, lv, lo, lgt, p["lstm_g"], batch, seq)
        x2d = _outproj(att, lstm, x2d, p["w_out"], p["ln1_g"], p["ln1_b"], tm)
        x2d = _ffn(x2d, p["w_g"], p["w_u"], p["conv_w"], p["conv_b"], p["w_d"], p["ln2_g"], p["ln2_b"], seq, tm)
    return x2d.reshape(batch, seq, D_MODEL)


def kernel(x_prompt, x_sample, w_in, gate_bias, lam_q1, lam_k1, lam_q2, lam_k2, att_norm_g, lstm_norm_g,
           w_out, ln1_g, ln1_b, w_gu, conv_w, conv_b, w_down, ln2_g, ln2_b):
    layers = [_prep_layer(l, w_in, gate_bias, lam_q1, lam_k1, lam_q2, lam_k2, att_norm_g, lstm_norm_g,
                          w_out, ln1_g, ln1_b, w_gu, conv_w, conv_b, w_down, ln2_g, ln2_b)
              for l in range(DEPTH)]
    return (_trunk(x_prompt, layers), _trunk(x_sample, layers))
```

```python
import functools
import math

import jax
import jax.numpy as jnp
from jax import lax
from jax.experimental import pallas as pl
from jax.experimental.pallas import tpu as pltpu

F32 = jnp.float32
BF16 = jnp.bfloat16

D_MODEL = 1024
DEPTH = 2
HEADS = 4
QK_DIM = 64
V_DIM = 128
ATT_WIDTH = HEADS * V_DIM
LSTM_WIDTH = HEADS * V_DIM
N_GATES = 4 * HEADS
D_FF = 2816
CHUNK = 128
ALPHA = (2 * DEPTH) ** 0.25
EPS = 1e-5
QK_SCALE = QK_DIM ** -0.5

C_AQ, C_AK, C_AV, C_LQ, C_LK, C_LV, C_LO, C_LG, C_END = 0, 512, 1024, 1536, 1792, 2048, 2560, 3072, 3088

LANES = 128
MXU_TILE = 256
VMEM_LIMIT = 56 * 1024 * 1024

PROJ_BLOCK = 1024
TOKEN_BLOCK = 512
FF_BOUNDS = (0, 6 * MXU_TILE, D_FF)
K_TILE = MXU_TILE
SUB = 2
Q_BLOCK = SUB * K_TILE

NT_DIMS = (((1,), (1,)), ((), ()))
TN_DIMS = (((0,), (0,)), ((), ()))


def _params(sem):
    return pltpu.CompilerParams(dimension_semantics=sem, vmem_limit_bytes=VMEM_LIMIT)


def _layer_norm(z, g, b):
    mu = jnp.mean(z, axis=-1, keepdims=True)
    zc = z - mu
    var = jnp.mean(zc * zc, axis=-1, keepdims=True)
    return zc * lax.rsqrt(var + EPS) * g + b


def _inproj_kernel(x_ref, w_ref, wakt_ref, wkt_ref, wgt_ref, gb_ref,
                   aq_ref, akt_ref, av_ref, lq_ref, lkt_ref, lv_ref, lo_ref, lgt_ref):
    xb = x_ref[...].astype(BF16)

    def seg(a, b):
        return jnp.dot(xb, w_ref[:, a:b], preferred_element_type=F32)

    aq_ref[...] = (seg(C_AQ, C_AK) * QK_SCALE).astype(BF16)
    akt_ref[...] = lax.dot_general(wakt_ref[...], xb, NT_DIMS, preferred_element_type=F32).astype(BF16)
    av_ref[...] = seg(C_AV, C_LQ).astype(BF16)
    lq_ref[...] = seg(C_LQ, C_LK).astype(BF16)
    kt = lax.dot_general(wkt_ref[...], xb, NT_DIMS, preferred_element_type=F32)
    lkt_ref[...] = (kt * QK_SCALE).astype(BF16)
    lv_ref[...] = seg(C_LV, C_LO).astype(BF16)
    lo_ref[...] = seg(C_LO, C_LG).astype(BF16)
    g = lax.dot_general(wgt_ref[...], xb, NT_DIMS, preferred_element_type=F32) + gb_ref[...]
    row = lax.broadcasted_iota(jnp.int32, g.shape, 0)
    is_forget = ((row >> 2) & 1) == 1
    log_sig = jnp.minimum(g, 0.0) - jnp.log1p(jnp.exp(-jnp.abs(g)))
    lgt_ref[...] = jnp.where(is_forget, log_sig, g)


def _inproj(x2d, w_main, w_akt, w_kt, w_gt, gbias, tm):
    t = x2d.shape[0]
    grid = (t // tm,)
    row_blk = lambda n: pl.BlockSpec((tm, n), lambda i: (i, 0))
    full = lambda a: pl.BlockSpec(a.shape, lambda i: (0, 0))
    out_shapes = (
        jax.ShapeDtypeStruct((t, 512), BF16),
        jax.ShapeDtypeStruct((512, t), BF16),
        jax.ShapeDtypeStruct((t, ATT_WIDTH), BF16),
        jax.ShapeDtypeStruct((t, 256), BF16),
        jax.ShapeDtypeStruct((256, t), BF16),
        jax.ShapeDtypeStruct((t, LSTM_WIDTH), BF16),
        jax.ShapeDtypeStruct((t, LSTM_WIDTH), BF16),
        jax.ShapeDtypeStruct((N_GATES, t), F32),
    )
    out_specs = (
        row_blk(512), pl.BlockSpec((512, tm), lambda i: (0, i)), row_blk(ATT_WIDTH), row_blk(256),
        pl.BlockSpec((256, tm), lambda i: (0, i)),
        row_blk(LSTM_WIDTH), row_blk(LSTM_WIDTH),
        pl.BlockSpec((N_GATES, tm), lambda i: (0, i)),
    )
    return pl.pallas_call(
        _inproj_kernel,
        grid=grid,
        in_specs=[row_blk(D_MODEL), full(w_main), full(w_akt), full(w_kt), full(w_gt), full(gbias)],
        out_specs=out_specs,
        out_shape=out_shapes,
        compiler_params=_params(("parallel",)),
        name="inproj",
    )(x2d, w_main, w_akt, w_kt, w_gt, gbias)


def _alibi_slope(h):
    return jnp.where(h == 0, 0.25, jnp.where(h == 1, 0.0625, jnp.where(h == 2, 0.015625, 0.00390625))).astype(F32)


def _attn_kernel(q_ref, k_ref, v_ref, lq1_ref, lk1_ref, lq2_ref, lk2_ref, g_ref, o_ref,
                 kcat_ref, vaug_ref, qv_ref, sa_ref, sb_ref, *, seq, n_blocks, lam_init):
    n_tiles = seq // K_TILE
    nq = seq // Q_BLOCK
    step = pl.program_id(0)
    blk = jnp.minimum(step, n_blocks - 1)
    head_idx = blk // nq
    h = head_idx % HEADS
    qi = blk % nq
    prev_head_idx = jnp.maximum(step - 1, 0) // nq
    slope = _alibi_slope(h)

    @pl.when(step == 0)
    def _zero_scores():
        sb_ref[...] = jnp.zeros_like(sb_ref)

    @pl.when((qi == 0) & (step < n_blocks))
    def _build_keys():
        kt = k_ref[...]
        frow = lax.broadcasted_iota(jnp.int32, (LANES, seq), 0)
        j = lax.broadcasted_iota(jnp.int32, (LANES, seq), 1)
        zero = jnp.zeros_like(kt)
        k1 = jnp.where(frow < QK_DIM, kt, zero)
        k2 = jnp.where(frow >= QK_DIM, kt, zero)
        kpos = jnp.where(frow < 2, 1.0,
                         jnp.where(frow == 2, (j >> 6).astype(F32),
                                   jnp.where(frow == 3, (j & 63).astype(F32), 0.0))).astype(BF16)
        for t in range(n_tiles):
            cols = slice(t * K_TILE, (t + 1) * K_TILE)
            kcat_ref[t, 0:LANES, 0:K_TILE] = k1[:, cols]
            kcat_ref[t, LANES:2 * LANES, 0:K_TILE] = kpos[:, cols]
            kcat_ref[t, 0:LANES, K_TILE:2 * K_TILE] = k2[:, cols]
            kcat_ref[t, LANES:2 * LANES, K_TILE:2 * K_TILE] = kpos[:, cols]
        slot = head_idx % 2
        vaug_ref[slot, :, 0:V_DIM] = v_ref[...]
        vaug_ref[slot, :, V_DIM:2 * V_DIM] = jnp.ones((seq, V_DIM), BF16)

    def block_step(s_new, s_old):
        q = q_ref[...]
        lane = lax.broadcasted_iota(jnp.int32, (Q_BLOCK, LANES), 1)
        row = lax.broadcasted_iota(jnp.int32, (Q_BLOCK, LANES), 0)
        i = qi * Q_BLOCK + row
        feat = slope * jnp.where(lane == 0, -64.0 * (i >> 6).astype(F32),
                                 jnp.where(lane == 1, -(i & 63).astype(F32),
                                           jnp.where(lane == 2, 64.0, jnp.where(lane == 3, 1.0, 0.0))))
        sub = row // K_TILE
        for var in range(SUB + 2):
            if var == 0:
                coef = 1.0
            elif var == SUB + 1:
                coef = -1.0
            else:
                coef = jnp.where(sub > var - 1, 1.0, jnp.where(sub == var - 1, 0.0, -1.0))
            qv_ref[var, :, 0:LANES] = q
            qv_ref[var, :, LANES:2 * LANES] = (coef * feat).astype(BF16)
        for t in range(n_tiles):
            sel = jnp.clip(t - SUB * qi + 1, 0, SUB + 1)
            s = jnp.dot(qv_ref[sel], kcat_ref[t], preferred_element_type=F32)
            s_new[0, t] = s[:, 0:K_TILE]
            s_new[1, t] = s[:, K_TILE:2 * K_TILE]
        r = lax.broadcasted_iota(jnp.int32, (K_TILE, K_TILE), 0)
        c = lax.broadcasted_iota(jnp.int32, (K_TILE, K_TILE), 1)
        diag_bias = -slope * jnp.abs(r - c).astype(F32)
        for sb in range(SUB):
            rows = slice(sb * K_TILE, (sb + 1) * K_TILE)
            for m in range(2):
                s_new[m, SUB * qi + sb, rows, :] = s_new[m, SUB * qi + sb, rows, :] + diag_bias

        vslot = prev_head_idx % 2
        row_max = []
        for m in range(2):
            mx = s_old[m, 0]
            for t in range(1, n_tiles):
                mx = jnp.maximum(mx, s_old[m, t])
            row_max.append(jnp.broadcast_to(jnp.max(mx, axis=1, keepdims=True), (Q_BLOCK, K_TILE)))
        acc = jnp.zeros((2 * Q_BLOCK, 2 * V_DIM), F32)
        for t in range(n_tiles):
            p = jnp.concatenate([jnp.exp(s_old[m, t] - row_max[m]).astype(BF16) for m in range(2)], axis=0)
            acc = acc + jnp.dot(p, vaug_ref[vslot, t * K_TILE:(t + 1) * K_TILE, :],
                                preferred_element_type=F32)
        outs = [acc[m * Q_BLOCK:(m + 1) * Q_BLOCK, 0:V_DIM] / acc[m * Q_BLOCK:(m + 1) * Q_BLOCK, V_DIM:2 * V_DIM]
                for m in range(2)]
        lam = (jnp.exp(jnp.sum(lq1_ref[...] * lk1_ref[...], keepdims=True))
               - jnp.exp(jnp.sum(lq2_ref[...] * lk2_ref[...], keepdims=True)) + lam_init)
        att = outs[0] - lam * outs[1]
        att = att * lax.rsqrt(jnp.mean(att * att, axis=-1, keepdims=True) + EPS)
        o_ref[...] = (att * g_ref[...] * (1.0 - lam_init)).astype(o_ref.dtype)

    @pl.when(step % 2 == 0)
    def _even():
        block_step(sa_ref, sb_ref)

    @pl.when(step % 2 == 1)
    def _odd():
        block_step(sb_ref, sa_ref)


def _attention(aq, akt, av, lq1, lk1, lq2, lk2, att_g, batch, seq, lam_init):
    t = batch * seq
    nq = seq // Q_BLOCK
    n_tiles = seq // K_TILE
    n_blocks = batch * HEADS * nq

    def cur(i):
        blk = jnp.minimum(i, n_blocks - 1)
        return blk // (HEADS * nq), (blk // nq) % HEADS, blk % nq

    def prev(i):
        blk = jnp.maximum(i - 1, 0)
        return blk // (HEADS * nq), (blk // nq) % HEADS, blk % nq

    def q_map(i):
        b, h, q = cur(i)
        return b * nq + q, h

    def kt_map(i):
        b, h, _ = cur(i)
        return h, b

    def v_map(i):
        b, h, _ = cur(i)
        return b, h

    def out_map(i):
        b, h, q = prev(i)
        return b * nq + q, h

    small = pl.BlockSpec((1, QK_DIM), lambda i: (0, 0))
    kernel = functools.partial(_attn_kernel, seq=seq, n_blocks=n_blocks, lam_init=lam_init)
    score_buf = pltpu.VMEM((2, n_tiles, Q_BLOCK, K_TILE), F32)
    return pl.pallas_call(
        kernel,
        grid=(n_blocks + 1,),
        in_specs=[
            pl.BlockSpec((Q_BLOCK, LANES), q_map),
            pl.BlockSpec((LANES, seq), kt_map),
            pl.BlockSpec((seq, V_DIM), v_map),
            small, small, small, small,
            pl.BlockSpec((1, V_DIM), lambda i: (0, prev(i)[1])),
        ],
        out_specs=pl.BlockSpec((Q_BLOCK, V_DIM), out_map),
        out_shape=jax.ShapeDtypeStruct((t, ATT_WIDTH), BF16),
        scratch_shapes=[
            pltpu.VMEM((n_tiles, 2 * LANES, 2 * K_TILE), BF16),
            pltpu.VMEM((2, seq, 2 * V_DIM), BF16),
            pltpu.VMEM((SUB + 2, Q_BLOCK, 2 * LANES), BF16),
            score_buf, score_buf,
        ],
        compiler_params=_params(("arbitrary",)),
        name="diff_attention",
    )(aq, akt, av, lq1, lk1, lq2, lk2, att_g)


def _split3(x):
    hi = x.astype(BF16).astype(F32)
    r1 = x - hi
    mid = r1.astype(BF16).astype(F32)
    lo = (r1 - mid).astype(BF16).astype(F32)
    return hi, mid, lo


def _chunk_scan(x, pos, combine, fill, reverse, seq):
    k = 1
    while k < CHUNK:
        if reverse:
            shifted = pltpu.roll(x, seq - k, 1)
            ok = pos < CHUNK - k
        else:
            shifted = pltpu.roll(x, k, 1)
            ok = pos >= k
        x = combine(x, jnp.where(ok, shifted, fill))
        k *= 2
    return x


def _mlstm_kernel(q_ref, kt_ref, v_ref, o_ref_in, gt_ref, g_ref, out_ref,
                  u_ref, bs_ref, cm_ref, hacc_ref, kz_ref, c_ref, cext_ref, lhs_ref, rhs_ref, *, seq):
    L = CHUNK
    nc = seq // L

    gts = gt_ref[...]
    row = lax.broadcasted_iota(jnp.int32, gts.shape, 0)
    pos = lax.broadcasted_iota(jnp.int32, gts.shape, 1) & (L - 1)
    psum = _chunk_scan(gts, pos, jnp.add, 0.0, False, seq)
    ssum = _chunk_scan(gts, pos, jnp.add, 0.0, True, seq)
    bsum = jnp.where(row < 2 * HEADS, psum, ssum)
    u = gts - pltpu.roll(bsum, N_GATES - HEADS, 0)
    cmax = jnp.where(row < 2 * HEADS,
                     _chunk_scan(u, pos, jnp.maximum, -jnp.inf, False, seq),
                     _chunk_scan(u, pos, jnp.maximum, -jnp.inf, True, seq))
    u_ref[...] = u
    bs_ref[...] = bsum
    cm_ref[...] = cmax

    hacc_ref[...] = jnp.zeros_like(hacc_ref)
    kz_ref[...] = jnp.zeros_like(kz_ref)
    c_ref[...] = jnp.zeros_like(c_ref)
    cext_ref[...] = jnp.zeros_like(cext_ref)
    lrow = lax.broadcasted_iota(jnp.int32, (16, L), 0)
    lhs_const = jnp.where(lrow < 3, 1.0, 0.0).astype(F32)
    rrow = lax.broadcasted_iota(jnp.int32, (16, 3 * L), 0)
    rcol = lax.broadcasted_iota(jnp.int32, (16, 3 * L), 1) >> 7
    rhs_const = jnp.where(((rrow >= 3) & (rrow < 6)) | ((rrow >= 6) & (rrow < 9) & (rcol == 2)),
                          1.0, 0.0).astype(F32)
    for n in range(2 * HEADS):
        lhs_ref[n] = lhs_const
        rhs_ref[n] = rhs_const

    jj = lax.broadcasted_iota(jnp.int32, (L, L), 0)
    ss = lax.broadcasted_iota(jnp.int32, (L, L), 1)
    masks = (ss <= jj, ss >= jj)
    ones_v = jnp.ones((L, V_DIM), BF16)

    def chunk_step(d, h, c, m_prev):
        n = d * HEADS + h
        c0 = pl.multiple_of(c * L, L)
        cols = pl.ds(c0, L)
        hrows = slice(h * QK_DIM, (h + 1) * QK_DIM)
        hcols = slice(h * V_DIM, (h + 1) * V_DIM)
        u_row = u_ref[8 * d + h:8 * d + h + 1, cols]
        b_row = bs_ref[8 * d + HEADS + h:8 * d + HEADS + h + 1, cols]
        cm_row = cm_ref[8 * d + h:8 * d + h + 1, cols]
        umax = jnp.max(u_row, axis=1, keepdims=True)
        g_tot = b_row[:, L - 1:L] if d == 0 else b_row[:, 0:1]
        mx_row = jnp.maximum(m_prev, cm_row)

        for r0, piece in ((3, -mx_row), (6, -b_row)):
            for k, part in enumerate(_split3(piece)):
                lhs_ref[n, r0 + k:r0 + k + 1, :] = part
        for k, part in enumerate(_split3(u_row)):
            rhs_ref[n, k:k + 1, 0:L] = part
        for k, part in enumerate(_split3(m_prev)):
            rhs_ref[n, k:k + 1, L:2 * L] = jnp.broadcast_to(part, (1, L))
        outer = lax.dot_general(lhs_ref[n].astype(BF16), rhs_ref[n].astype(BF16), TN_DIMS,
                                preferred_element_type=F32)
        e = jnp.exp(outer)
        p = jnp.where(masks[d], e[:, 0:L], 0.0)
        e_inter = e[:, L:2 * L]
        clamp = e[:, 2 * L:3 * L]

        q = q_ref[cols, :]
        kh = kt_ref[hrows, cols]
        kz_ref[n, hrows, :] = kh
        qk = jnp.dot(q, kz_ref[n], preferred_element_type=F32)
        sqk = (qk * p).astype(BF16)
        vaug = jnp.concatenate([v_ref[cols, hcols], ones_v], axis=1)
        ke = (kh.astype(F32) * jnp.exp(u_row - umax)).astype(BF16)
        both = jnp.dot(jnp.concatenate([sqk, ke], axis=0), vaug, preferred_element_type=F32)
        intra = both[0:L]
        c_loc = both[L:L + QK_DIM]
        inter = jnp.dot(q, cext_ref[n], preferred_element_type=F32)
        nd = intra + jnp.concatenate([e_inter, e_inter], axis=1) * inter
        hout = nd[:, 0:V_DIM] / jnp.maximum(jnp.abs(nd[:, V_DIM:2 * V_DIM]), clamp)
        hacc_ref[cols, hcols] = hacc_ref[cols, hcols] + hout

        m_loc = g_tot + umax
        m_new = jnp.maximum(g_tot + m_prev, m_loc)
        a = jnp.exp(g_tot + m_prev - m_new)
        bb = jnp.exp(m_loc - m_new)
        c_new = a * c_ref[n] + bb * c_loc
        c_ref[n] = c_new
        cext_ref[n, hrows, :] = c_new.astype(BF16)
        return m_new

    def body(it, carry):
        new = []
        for d in range(2):
            c = it if d == 0 else nc - 1 - it
            for h in range(HEADS):
                new.append(chunk_step(d, h, c, carry[d * HEADS + h]))
        return tuple(new)

    zero = jnp.zeros((1, 1), F32)
    lax.fori_loop(0, nc, body, (zero,) * (2 * HEADS), unroll=2)

    for h in range(HEADS):
        hcols = slice(h * V_DIM, (h + 1) * V_DIM)
        hs = hacc_ref[:, hcols]
        hn = hs * lax.rsqrt(jnp.mean(hs * hs, axis=-1, keepdims=True) + EPS) * g_ref[:, hcols]
        out_ref[:, hcols] = (jax.nn.sigmoid(o_ref_in[:, hcols].astype(F32)) * hn).astype(out_ref.dtype)


def _mlstm(lq, lkt, lv, lo, lgt, lstm_g, batch, seq):
    t = batch * seq
    kernel = functools.partial(_mlstm_kernel, seq=seq)
    L = CHUNK
    n_rec = 2 * HEADS
    gate_rows = pltpu.VMEM((N_GATES, seq), F32)
    return pl.pallas_call(
        kernel,
        grid=(batch,),
        in_specs=[
            pl.BlockSpec((seq, 256), lambda b: (b, 0)),
            pl.BlockSpec((256, seq), lambda b: (0, b)),
            pl.BlockSpec((seq, LSTM_WIDTH), lambda b: (b, 0)),
            pl.BlockSpec((seq, LSTM_WIDTH), lambda b: (b, 0)),
            pl.BlockSpec((N_GATES, seq), lambda b: (0, b)),
            pl.BlockSpec((1, LSTM_WIDTH), lambda b: (0, 0)),
        ],
        out_specs=pl.BlockSpec((seq, LSTM_WIDTH), lambda b: (b, 0)),
        out_shape=jax.ShapeDtypeStruct((t, LSTM_WIDTH), BF16),
        scratch_shapes=[
            gate_rows,
            gate_rows,
            gate_rows,
            pltpu.VMEM((seq, LSTM_WIDTH), F32),
            pltpu.VMEM((n_rec, 256, L), BF16),
            pltpu.VMEM((n_rec, QK_DIM, 2 * V_DIM), F32),
            pltpu.VMEM((n_rec, 256, 2 * V_DIM), BF16),
            pltpu.VMEM((n_rec, 16, L), F32),
            pltpu.VMEM((n_rec, 16, 3 * L), F32),
        ],
        compiler_params=_params(("parallel",)),
        name="mlstm",
    )(lq, lkt, lv, lo, lgt, lstm_g)


def _outproj_kernel(att_ref, lstm_ref, x_ref, w_ref, g_ref, b_ref, o_ref):
    y = jnp.dot(att_ref[...], w_ref[0:ATT_WIDTH, :], preferred_element_type=F32)
    y = y + jnp.dot(lstm_ref[...], w_ref[ATT_WIDTH:ATT_WIDTH + LSTM_WIDTH, :], preferred_element_type=F32)
    o_ref[...] = _layer_norm(ALPHA * x_ref[...] + y, g_ref[...], b_ref[...])


def _outproj(att, lstm, x2d, w_out, g, b, tm):
    t = x2d.shape[0]
    row_blk = lambda n: pl.BlockSpec((tm, n), lambda i: (i, 0))
    full = lambda a: pl.BlockSpec(a.shape, lambda i: (0, 0))
    return pl.pallas_call(
        _outproj_kernel,
        grid=(t // tm,),
        in_specs=[row_blk(ATT_WIDTH), row_blk(LSTM_WIDTH), row_blk(D_MODEL), full(w_out), full(g), full(b)],
        out_specs=row_blk(D_MODEL),
        out_shape=jax.ShapeDtypeStruct((t, D_MODEL), F32),
        compiler_params=_params(("parallel",)),
        name="outproj_ln1",
    )(att, lstm, x2d, w_out, g, b)


HALO = 8


def _ffn_kernel(x_ref, xp_ref, xn_ref, wg_ref, wu_ref, cw_ref, cb_ref, wd_ref, g_ref, b_ref, o_ref,
                gs_ref, *, seq, tm):
    i = pl.program_id(0)
    xb = x_ref[...].astype(BF16)
    has_prev = ((i * tm) % seq != 0).astype(F32)
    has_next = (((i + 1) * tm) % seq != 0).astype(F32)
    halo = jnp.concatenate([xp_ref[...] * has_prev, xn_ref[...] * has_next], axis=0).astype(BF16)
    acc = None
    for j in range(len(FF_BOUNDS) - 1):
        cs = slice(FF_BOUNDS[j], FF_BOUNDS[j + 1])
        gs_ref[HALO:HALO + tm, cs] = jnp.dot(xb, wg_ref[:, cs], preferred_element_type=F32)
        gh = jnp.dot(halo, wg_ref[:, cs], preferred_element_type=F32)
        gs_ref[0:HALO, cs] = gh[0:HALO]
        gs_ref[HALO + tm:2 * HALO + tm, cs] = gh[HALO:2 * HALO]
        conv = (cb_ref[:, cs]
                + gs_ref[HALO - 1:HALO - 1 + tm, cs] * cw_ref[0:1, cs]
                + gs_ref[HALO:HALO + tm, cs] * cw_ref[1:2, cs]
                + gs_ref[HALO + 1:HALO + 1 + tm, cs] * cw_ref[2:3, cs])
        up = jnp.dot(xb, wu_ref[:, cs], preferred_element_type=F32)
        gelu = 0.5 * conv * (lax.erf(conv * (1.0 / math.sqrt(2.0))) + 1.0)
        hmid = (gelu * up).astype(BF16)
        part = jnp.dot(hmid, wd_ref[cs, :], preferred_element_type=F32)
        acc = part if acc is None else acc + part
    o_ref[...] = _layer_norm(ALPHA * x_ref[...] + acc, g_ref[...], b_ref[...])


def _ffn(x2d, w_g, w_u, conv_w, conv_b, w_d, g, b, seq, tm):
    t = x2d.shape[0]
    hb = tm // HALO
    last_hblk = t // HALO - 1
    kernel = functools.partial(_ffn_kernel, seq=seq, tm=tm)
    resident = lambda a: pl.BlockSpec(a.shape, lambda i: (0, 0), pipeline_mode=pl.Buffered(1))
    return pl.pallas_call(
        kernel,
        grid=(t // tm,),
        in_specs=[
            pl.BlockSpec((tm, D_MODEL), lambda i: (i, 0)),
            pl.BlockSpec((HALO, D_MODEL), lambda i: (jnp.maximum(i * hb - 1, 0), 0)),
            pl.BlockSpec((HALO, D_MODEL), lambda i: (jnp.minimum((i + 1) * hb, last_hblk), 0)),
            resident(w_g), resident(w_u), resident(conv_w), resident(conv_b), resident(w_d),
            resident(g), resident(b),
        ],
        out_specs=pl.BlockSpec((tm, D_MODEL), lambda i: (i, 0)),
        out_shape=jax.ShapeDtypeStruct((t, D_MODEL), F32),
        scratch_shapes=[
            pltpu.VMEM((tm + 2 * HALO, D_FF), F32),
        ],
        compiler_params=_params(("parallel",)),
        name="conv_ffn_ln2",
    )(x2d, x2d, x2d, w_g, w_u, conv_w, conv_b, w_d, g, b)


def _prep_layer(l, w_in, gate_bias, lam_q1, lam_k1, lam_q2, lam_k2, att_norm_g, lstm_norm_g, w_out,
                ln1_g, ln1_b, w_gu, conv_w, conv_b, w_down, ln2_g, ln2_b):
    wl = w_in[l]
    row = lambda a: a.reshape(1, -1).astype(F32)
    return dict(
        w_main=wl[:, :C_LG].astype(BF16),
        w_akt=wl[:, C_AK:C_AV].T.astype(BF16),
        w_kt=wl[:, C_LK:C_LV].T.astype(BF16),
        w_gt=wl[:, C_LG:C_END].T.astype(BF16),
        gbias=gate_bias[l].reshape(N_GATES, 1).astype(F32),
        lq1=row(lam_q1[l]), lk1=row(lam_k1[l]), lq2=row(lam_q2[l]), lk2=row(lam_k2[l]),
        att_g=row(att_norm_g[l]), lstm_g=row(lstm_norm_g[l]),
        w_out=w_out[l].astype(BF16), ln1_g=row(ln1_g[l]), ln1_b=row(ln1_b[l]),
        w_g=w_gu[l][:, :D_FF].astype(BF16), w_u=w_gu[l][:, D_FF:].astype(BF16),
        conv_w=conv_w[l].astype(F32), conv_b=row(conv_b[l]),
        w_d=w_down[l].astype(BF16), ln2_g=row(ln2_g[l]), ln2_b=row(ln2_b[l]),
    )


def _trunk(x, layers, tm_proj=PROJ_BLOCK, tm=TOKEN_BLOCK):
    batch, seq, _ = x.shape
    x2d = x.reshape(batch * seq, D_MODEL)
    for l, p in enumerate(layers):
        lam_init = 0.8 - 0.6 * math.exp(-0.3 * l)
        aq, akt, av, lq, lkt, lv, lo, lgt = _inproj(x2d, p["w_main"], p["w_akt"], p["w_kt"], p["w_gt"],
                                                    p["gbias"], tm_proj)
        att = _attention(aq, akt, av, p["lq1"], p["lk1"], p["lq2"], p["lk2"], p["att_g"], batch, seq, lam_init)
        lstm = _mlstm(lq, lkt, lv, lo, lgt, p["lstm_g"], batch, seq)
        x2d = _outproj(att, lstm, x2d, p["w_out"], p["ln1_g"], p["ln1_b"], tm_proj)
        x2d = _ffn(x2d, p["w_g"], p["w_u"], p["conv_w"], p["conv_b"], p["w_d"], p["ln2_g"], p["ln2_b"], seq, tm)
    return x2d.reshape(batch, seq, D_MODEL)


def kernel(x_prompt, x_sample, w_in, gate_bias, lam_q1, lam_k1, lam_q2, lam_k2, att_norm_g, lstm_norm_g,
           w_out, ln1_g, ln1_b, w_gu, conv_w, conv_b, w_down, ln2_g, ln2_b):
    layers = [_prep_layer(l, w_in, gate_bias, lam_q1, lam_k1, lam_q2, lam_k2, att_norm_g, lstm_norm_g,
                          w_out, ln1_g, ln1_b, w_gu, conv_w, conv_b, w_down, ln2_g, ln2_b)
              for l in range(DEPTH)]
    return (_trunk(x_prompt, layers), _trunk(x_sample, layers))
```

```python
import functools
import math

import jax
import jax.numpy as jnp
from jax import lax
from jax.experimental import pallas as pl
from jax.experimental.pallas import tpu as pltpu

F32 = jnp.float32
BF16 = jnp.bfloat16

D_MODEL = 1024
DEPTH = 2
HEADS = 4
QK_DIM = 64
V_DIM = 128
ATT_WIDTH = HEADS * V_DIM
LSTM_WIDTH = HEADS * V_DIM
N_GATES = 4 * HEADS
D_FF = 2816
CHUNK = 128
ALPHA = (2 * DEPTH) ** 0.25
EPS = 1e-5
QK_SCALE = QK_DIM ** -0.5

C_AQ, C_AK, C_AV, C_LQ, C_LK, C_LV, C_LO, C_LG, C_END = 0, 512, 1024, 1536, 1792, 2048, 2560, 3072, 3088

LANES = 128
MXU_TILE = 256
VMEM_LIMIT = 56 * 1024 * 1024

PROJ_BLOCK = 1024
TOKEN_BLOCK = 512
FF_BOUNDS = (0, 6 * MXU_TILE, D_FF)
K_TILE = MXU_TILE
SCORE_BUF_BYTES = 16 * 1024 * 1024


def _attn_sub_blocks(seq):
    return min(SCORE_BUF_BYTES // (2 * K_TILE * seq * 4), seq // K_TILE)

NT_DIMS = (((1,), (1,)), ((), ()))
TN_DIMS = (((0,), (0,)), ((), ()))


def _params(sem):
    return pltpu.CompilerParams(dimension_semantics=sem, vmem_limit_bytes=VMEM_LIMIT)


def _layer_norm(z, g, b):
    mu = jnp.mean(z, axis=-1, keepdims=True)
    zc = z - mu
    var = jnp.mean(zc * zc, axis=-1, keepdims=True)
    return zc * lax.rsqrt(var + EPS) * g + b


def _inproj_kernel(x_ref, w_ref, wakt_ref, wkt_ref, wgt_ref, gb_ref,
                   aq_ref, akt_ref, av_ref, lq_ref, lkt_ref, lv_ref, lo_ref, lgt_ref):
    xb = x_ref[...].astype(BF16)

    def seg(a, b):
        return jnp.dot(xb, w_ref[:, a:b], preferred_element_type=F32)

    aq_ref[...] = (seg(C_AQ, C_AK) * QK_SCALE).astype(BF16)
    akt_ref[...] = lax.dot_general(wakt_ref[...], xb, NT_DIMS, preferred_element_type=F32).astype(BF16)
    av_ref[...] = seg(C_AV, C_LQ).astype(BF16)
    lq_ref[...] = seg(C_LQ, C_LK).astype(BF16)
    kt = lax.dot_general(wkt_ref[...], xb, NT_DIMS, preferred_element_type=F32)
    lkt_ref[...] = (kt * QK_SCALE).astype(BF16)
    lv_ref[...] = seg(C_LV, C_LO).astype(BF16)
    lo_ref[...] = seg(C_LO, C_LG).astype(BF16)
    g = lax.dot_general(wgt_ref[...], xb, NT_DIMS, preferred_element_type=F32) + gb_ref[...]
    row = lax.broadcasted_iota(jnp.int32, g.shape, 0)
    is_forget = ((row >> 2) & 1) == 1
    log_sig = jnp.minimum(g, 0.0) - jnp.log1p(jnp.exp(-jnp.abs(g)))
    lgt_ref[...] = jnp.where(is_forget, log_sig, g)


def _inproj(x2d, w_main, w_akt, w_kt, w_gt, gbias, tm):
    t = x2d.shape[0]
    grid = (t // tm,)
    row_blk = lambda n: pl.BlockSpec((tm, n), lambda i: (i, 0))
    full = lambda a: pl.BlockSpec(a.shape, lambda i: (0, 0))
    out_shapes = (
        jax.ShapeDtypeStruct((t, 512), BF16),
        jax.ShapeDtypeStruct((512, t), BF16),
        jax.ShapeDtypeStruct((t, ATT_WIDTH), BF16),
        jax.ShapeDtypeStruct((t, 256), BF16),
        jax.ShapeDtypeStruct((256, t), BF16),
        jax.ShapeDtypeStruct((t, LSTM_WIDTH), BF16),
        jax.ShapeDtypeStruct((t, LSTM_WIDTH), BF16),
        jax.ShapeDtypeStruct((N_GATES, t), F32),
    )
    out_specs = (
        row_blk(512), pl.BlockSpec((512, tm), lambda i: (0, i)), row_blk(ATT_WIDTH), row_blk(256),
        pl.BlockSpec((256, tm), lambda i: (0, i)),
        row_blk(LSTM_WIDTH), row_blk(LSTM_WIDTH),
        pl.BlockSpec((N_GATES, tm), lambda i: (0, i)),
    )
    return pl.pallas_call(
        _inproj_kernel,
        grid=grid,
        in_specs=[row_blk(D_MODEL), full(w_main), full(w_akt), full(w_kt), full(w_gt), full(gbias)],
        out_specs=out_specs,
        out_shape=out_shapes,
        compiler_params=_params(("parallel",)),
        name="inproj",
    )(x2d, w_main, w_akt, w_kt, w_gt, gbias)


def _alibi_slope(h):
    return jnp.where(h == 0, 0.25, jnp.where(h == 1, 0.0625, jnp.where(h == 2, 0.015625, 0.00390625))).astype(F32)


def _attn_kernel(q_ref, k_ref, v_ref, lq1_ref, lk1_ref, lq2_ref, lk2_ref, g_ref, o_ref,
                 kcat_ref, vaug_ref, qv_ref, sa_ref, sb_ref, *, seq, n_blocks, lam_init):
    SUB = _attn_sub_blocks(seq)
    Q_BLOCK = SUB * K_TILE
    n_tiles = seq // K_TILE
    nq = seq // Q_BLOCK
    step = pl.program_id(0)
    blk = jnp.minimum(step, n_blocks - 1)
    head_idx = blk // nq
    h = head_idx % HEADS
    qi = blk % nq
    prev_head_idx = jnp.maximum(step - 1, 0) // nq
    slope = _alibi_slope(h)

    @pl.when(step == 0)
    def _zero_scores():
        sb_ref[...] = jnp.zeros_like(sb_ref)

    @pl.when((qi == 0) & (step < n_blocks))
    def _build_keys():
        kt = k_ref[...]
        frow = lax.broadcasted_iota(jnp.int32, (LANES, seq), 0)
        j = lax.broadcasted_iota(jnp.int32, (LANES, seq), 1)
        zero = jnp.zeros_like(kt)
        k1 = jnp.where(frow < QK_DIM, kt, zero)
        k2 = jnp.where(frow >= QK_DIM, kt, zero)
        kpos = jnp.where(frow < 2, 1.0,
                         jnp.where(frow == 2, (j >> 6).astype(F32),
                                   jnp.where(frow == 3, (j & 63).astype(F32), 0.0))).astype(BF16)
        for t in range(n_tiles):
            cols = slice(t * K_TILE, (t + 1) * K_TILE)
            kcat_ref[t, 0:LANES, 0:K_TILE] = k1[:, cols]
            kcat_ref[t, LANES:2 * LANES, 0:K_TILE] = kpos[:, cols]
            kcat_ref[t, 0:LANES, K_TILE:2 * K_TILE] = k2[:, cols]
            kcat_ref[t, LANES:2 * LANES, K_TILE:2 * K_TILE] = kpos[:, cols]
        slot = head_idx % 2
        vaug_ref[slot, :, 0:V_DIM] = v_ref[...]
        vaug_ref[slot, :, V_DIM:2 * V_DIM] = jnp.ones((seq, V_DIM), BF16)

    def block_step(s_new, s_old):
        q = q_ref[...]
        lane = lax.broadcasted_iota(jnp.int32, (Q_BLOCK, LANES), 1)
        row = lax.broadcasted_iota(jnp.int32, (Q_BLOCK, LANES), 0)
        i = qi * Q_BLOCK + row
        feat = slope * jnp.where(lane == 0, -64.0 * (i >> 6).astype(F32),
                                 jnp.where(lane == 1, -(i & 63).astype(F32),
                                           jnp.where(lane == 2, 64.0, jnp.where(lane == 3, 1.0, 0.0))))
        sub = row // K_TILE
        for var in range(SUB + 2):
            if var == 0:
                coef = 1.0
            elif var == SUB + 1:
                coef = -1.0
            else:
                coef = jnp.where(sub > var - 1, 1.0, jnp.where(sub == var - 1, 0.0, -1.0))
            qv_ref[var, :, 0:LANES] = q
            qv_ref[var, :, LANES:2 * LANES] = (coef * feat).astype(BF16)
        for t in range(n_tiles):
            sel = jnp.clip(t - SUB * qi + 1, 0, SUB + 1)
            s = jnp.dot(qv_ref[sel], kcat_ref[t], preferred_element_type=F32)
            s_new[0, t] = s[:, 0:K_TILE]
            s_new[1, t] = s[:, K_TILE:2 * K_TILE]
        r = lax.broadcasted_iota(jnp.int32, (K_TILE, K_TILE), 0)
        c = lax.broadcasted_iota(jnp.int32, (K_TILE, K_TILE), 1)
        diag_bias = -slope * jnp.abs(r - c).astype(F32)
        for sb in range(SUB):
            rows = slice(sb * K_TILE, (sb + 1) * K_TILE)
            for m in range(2):
                s_new[m, SUB * qi + sb, rows, :] = s_new[m, SUB * qi + sb, rows, :] + diag_bias

        vslot = prev_head_idx % 2
        row_max = []
        for m in range(2):
            mx = s_old[m, 0]
            for t in range(1, n_tiles):
                mx = jnp.maximum(mx, s_old[m, t])
            row_max.append(jnp.broadcast_to(jnp.max(mx, axis=1, keepdims=True), (Q_BLOCK, K_TILE)))
        acc = jnp.zeros((2 * Q_BLOCK, 2 * V_DIM), F32)
        for t in range(n_tiles):
            p = jnp.concatenate([jnp.exp(s_old[m, t] - row_max[m]).astype(BF16) for m in range(2)], axis=0)
            acc = acc + jnp.dot(p, vaug_ref[vslot, t * K_TILE:(t + 1) * K_TILE, :],
                                preferred_element_type=F32)
        outs = [acc[m * Q_BLOCK:(m + 1) * Q_BLOCK, 0:V_DIM] / acc[m * Q_BLOCK:(m + 1) * Q_BLOCK, V_DIM:2 * V_DIM]
                for m in range(2)]
        lam = (jnp.exp(jnp.sum(lq1_ref[...] * lk1_ref[...], keepdims=True))
               - jnp.exp(jnp.sum(lq2_ref[...] * lk2_ref[...], keepdims=True)) + lam_init)
        att = outs[0] - lam * outs[1]
        att = att * lax.rsqrt(jnp.mean(att * att, axis=-1, keepdims=True) + EPS)
        o_ref[...] = (att * g_ref[...] * (1.0 - lam_init)).astype(o_ref.dtype)

    @pl.when(step % 2 == 0)
    def _even():
        block_step(sa_ref, sb_ref)

    @pl.when(step % 2 == 1)
    def _odd():
        block_step(sb_ref, sa_ref)


def _attention(aq, akt, av, lq1, lk1, lq2, lk2, att_g, batch, seq, lam_init):
    t = batch * seq
    SUB = _attn_sub_blocks(seq)
    Q_BLOCK = SUB * K_TILE
    nq = seq // Q_BLOCK
    n_tiles = seq // K_TILE
    n_blocks = batch * HEADS * nq

    def cur(i):
        blk = jnp.minimum(i, n_blocks - 1)
        return blk // (HEADS * nq), (blk // nq) % HEADS, blk % nq

    def prev(i):
        blk = jnp.maximum(i - 1, 0)
        return blk // (HEADS * nq), (blk // nq) % HEADS, blk % nq

    def q_map(i):
        b, h, q = cur(i)
        return b * nq + q, h

    def kt_map(i):
        b, h, _ = cur(i)
        return h, b

    def v_map(i):
        b, h, _ = cur(i)
        return b, h

    def out_map(i):
        b, h, q = prev(i)
        return b * nq + q, h

    small = pl.BlockSpec((1, QK_DIM), lambda i: (0, 0))
    kernel = functools.partial(_attn_kernel, seq=seq, n_blocks=n_blocks, lam_init=lam_init)
    score_buf = pltpu.VMEM((2, n_tiles, Q_BLOCK, K_TILE), F32)
    return pl.pallas_call(
        kernel,
        grid=(n_blocks + 1,),
        in_specs=[
            pl.BlockSpec((Q_BLOCK, LANES), q_map),
            pl.BlockSpec((LANES, seq), kt_map),
            pl.BlockSpec((seq, V_DIM), v_map),
            small, small, small, small,
            pl.BlockSpec((1, V_DIM), lambda i: (0, prev(i)[1])),
        ],
        out_specs=pl.BlockSpec((Q_BLOCK, V_DIM), out_map),
        out_shape=jax.ShapeDtypeStruct((t, ATT_WIDTH), BF16),
        scratch_shapes=[
            pltpu.VMEM((n_tiles, 2 * LANES, 2 * K_TILE), BF16),
            pltpu.VMEM((2, seq, 2 * V_DIM), BF16),
            pltpu.VMEM((SUB + 2, Q_BLOCK, 2 * LANES), BF16),
            score_buf, score_buf,
        ],
        compiler_params=_params(("arbitrary",)),
        name="diff_attention",
    )(aq, akt, av, lq1, lk1, lq2, lk2, att_g)


def _split3(x):
    hi = x.astype(BF16).astype(F32)
    r1 = x - hi
    mid = r1.astype(BF16).astype(F32)
    lo = (r1 - mid).astype(BF16).astype(F32)
    return hi, mid, lo


def _chunk_scan(x, pos, combine, fill, reverse, seq):
    k = 1
    while k < CHUNK:
        if reverse:
            shifted = pltpu.roll(x, seq - k, 1)
            ok = pos < CHUNK - k
        else:
            shifted = pltpu.roll(x, k, 1)
            ok = pos >= k
        x = combine(x, jnp.where(ok, shifted, fill))
        k *= 2
    return x


def _mlstm_kernel(q_ref, kt_ref, v_ref, o_ref_in, gt_ref, g_ref, out_ref,
                  u_ref, bs_ref, cm_ref, hacc_ref, kz_ref, c_ref, cext_ref, lhs_ref, rhs_ref, *, seq):
    L = CHUNK
    nc = seq // L

    gts = gt_ref[...]
    row = lax.broadcasted_iota(jnp.int32, gts.shape, 0)
    pos = lax.broadcasted_iota(jnp.int32, gts.shape, 1) & (L - 1)
    psum = _chunk_scan(gts, pos, jnp.add, 0.0, False, seq)
    ssum = _chunk_scan(gts, pos, jnp.add, 0.0, True, seq)
    bsum = jnp.where(row < 2 * HEADS, psum, ssum)
    u = gts - pltpu.roll(bsum, N_GATES - HEADS, 0)
    cmax = jnp.where(row < 2 * HEADS,
                     _chunk_scan(u, pos, jnp.maximum, -jnp.inf, False, seq),
                     _chunk_scan(u, pos, jnp.maximum, -jnp.inf, True, seq))
    u_ref[...] = u
    bs_ref[...] = bsum
    cm_ref[...] = cmax

    hacc_ref[...] = jnp.zeros_like(hacc_ref)
    kz_ref[...] = jnp.zeros_like(kz_ref)
    c_ref[...] = jnp.zeros_like(c_ref)
    cext_ref[...] = jnp.zeros_like(cext_ref)
    lrow = lax.broadcasted_iota(jnp.int32, (16, L), 0)
    lhs_const = jnp.where(lrow < 3, 1.0, 0.0).astype(F32)
    rrow = lax.broadcasted_iota(jnp.int32, (16, 3 * L), 0)
    rcol = lax.broadcasted_iota(jnp.int32, (16, 3 * L), 1) >> 7
    rhs_const = jnp.where(((rrow >= 3) & (rrow < 6)) | ((rrow >= 6) & (rrow < 9) & (rcol == 2)),
                          1.0, 0.0).astype(F32)
    for n in range(2 * HEADS):
        lhs_ref[n] = lhs_const
        rhs_ref[n] = rhs_const

    jj = lax.broadcasted_iota(jnp.int32, (L, L), 0)
    ss = lax.broadcasted_iota(jnp.int32, (L, L), 1)
    masks = (ss <= jj, ss >= jj)
    ones_v = jnp.ones((L, V_DIM), BF16)

    def chunk_step(d, h, c, m_prev):
        n = d * HEADS + h
        c0 = pl.multiple_of(c * L, L)
        cols = pl.ds(c0, L)
        hrows = slice(h * QK_DIM, (h + 1) * QK_DIM)
        zrows = slice((h % 2) * QK_DIM, (h % 2 + 1) * QK_DIM)
        hcols = slice(h * V_DIM, (h + 1) * V_DIM)
        u_row = u_ref[8 * d + h:8 * d + h + 1, cols]
        b_row = bs_ref[8 * d + HEADS + h:8 * d + HEADS + h + 1, cols]
        cm_row = cm_ref[8 * d + h:8 * d + h + 1, cols]
        umax = jnp.max(u_row, axis=1, keepdims=True)
        g_tot = b_row[:, L - 1:L] if d == 0 else b_row[:, 0:1]
        mx_row = jnp.maximum(m_prev, cm_row)

        for r0, piece in ((3, -mx_row), (6, -b_row)):
            for k, part in enumerate(_split3(piece)):
                lhs_ref[n, r0 + k:r0 + k + 1, :] = part
        for k, part in enumerate(_split3(u_row)):
            rhs_ref[n, k:k + 1, 0:L] = part
        for k, part in enumerate(_split3(m_prev)):
            rhs_ref[n, k:k + 1, L:2 * L] = jnp.broadcast_to(part, (1, L))
        outer = lax.dot_general(lhs_ref[n].astype(BF16), rhs_ref[n].astype(BF16), TN_DIMS,
                                preferred_element_type=F32)
        e = jnp.exp(outer)
        p = jnp.where(masks[d], e[:, 0:L], 0.0)
        e_inter = e[:, L:2 * L]
        clamp = e[:, 2 * L:3 * L]

        q = q_ref[cols, (h // 2) * LANES:(h // 2 + 1) * LANES]
        kh = kt_ref[hrows, cols]
        kz_ref[n, zrows, :] = kh
        qk = jnp.dot(q, kz_ref[n], preferred_element_type=F32)
        sqk = (qk * p).astype(BF16)
        vaug = jnp.concatenate([v_ref[cols, hcols], ones_v], axis=1)
        ke = (kh.astype(F32) * jnp.exp(u_row - umax)).astype(BF16)
        both = jnp.dot(jnp.concatenate([sqk, ke], axis=0), vaug, preferred_element_type=F32)
        intra = both[0:L]
        c_loc = both[L:L + QK_DIM]
        inter = jnp.dot(q, cext_ref[n], preferred_element_type=F32)
        nd = intra + jnp.concatenate([e_inter, e_inter], axis=1) * inter
        hout = nd[:, 0:V_DIM] / jnp.maximum(jnp.abs(nd[:, V_DIM:2 * V_DIM]), clamp)
        hacc_ref[cols, hcols] = hacc_ref[cols, hcols] + hout

        m_loc = g_tot + umax
        m_new = jnp.maximum(g_tot + m_prev, m_loc)
        a = jnp.exp(g_tot + m_prev - m_new)
        bb = jnp.exp(m_loc - m_new)
        c_new = a * c_ref[n] + bb * c_loc
        c_ref[n] = c_new
        cext_ref[n, zrows, :] = c_new.astype(BF16)
        return m_new

    def body(it, carry):
        new = []
        for d in range(2):
            c = it if d == 0 else nc - 1 - it
            for h in range(HEADS):
                new.append(chunk_step(d, h, c, carry[d * HEADS + h]))
        return tuple(new)

    zero = jnp.zeros((1, 1), F32)
    lax.fori_loop(0, nc, body, (zero,) * (2 * HEADS), unroll=2)

    for h in range(HEADS):
        hcols = slice(h * V_DIM, (h + 1) * V_DIM)
        hs = hacc_ref[:, hcols]
        hn = hs * lax.rsqrt(jnp.mean(hs * hs, axis=-1, keepdims=True) + EPS) * g_ref[:, hcols]
        out_ref[:, hcols] = (jax.nn.sigmoid(o_ref_in[:, hcols].astype(F32)) * hn).astype(out_ref.dtype)


def _mlstm(lq, lkt, lv, lo, lgt, lstm_g, batch, seq):
    t = batch * seq
    kernel = functools.partial(_mlstm_kernel, seq=seq)
    L = CHUNK
    n_rec = 2 * HEADS
    gate_rows = pltpu.VMEM((N_GATES, seq), F32)
    return pl.pallas_call(
        kernel,
        grid=(batch,),
        in_specs=[
            pl.BlockSpec((seq, 256), lambda b: (b, 0)),
            pl.BlockSpec((256, seq), lambda b: (0, b)),
            pl.BlockSpec((seq, LSTM_WIDTH), lambda b: (b, 0)),
            pl.BlockSpec((seq, LSTM_WIDTH), lambda b: (b, 0)),
            pl.BlockSpec((N_GATES, seq), lambda b: (0, b)),
            pl.BlockSpec((1, LSTM_WIDTH), lambda b: (0, 0)),
        ],
        out_specs=pl.BlockSpec((seq, LSTM_WIDTH), lambda b: (b, 0)),
        out_shape=jax.ShapeDtypeStruct((t, LSTM_WIDTH), BF16),
        scratch_shapes=[
            gate_rows,
            gate_rows,
            gate_rows,
            pltpu.VMEM((seq, LSTM_WIDTH), F32),
            pltpu.VMEM((n_rec, LANES, L), BF16),
            pltpu.VMEM((n_rec, QK_DIM, 2 * V_DIM), F32),
            pltpu.VMEM((n_rec, LANES, 2 * V_DIM), BF16),
            pltpu.VMEM((n_rec, 16, L), F32),
            pltpu.VMEM((n_rec, 16, 3 * L), F32),
        ],
        compiler_params=_params(("parallel",)),
        name="mlstm",
    )(lq, lkt, lv, lo, lgt, lstm_g)


def _outproj_kernel(att_ref, lstm_ref, x_ref, w_ref, g_ref, b_ref, o_ref):
    y = jnp.dot(att_ref[...], w_ref[0:ATT_WIDTH, :], preferred_element_type=F32)
    y = y + jnp.dot(lstm_ref[...], w_ref[ATT_WIDTH:ATT_WIDTH + LSTM_WIDTH, :], preferred_element_type=F32)
    o_ref[...] = _layer_norm(ALPHA * x_ref[...] + y, g_ref[...], b_ref[...])


def _outproj(att, lstm, x2d, w_out, g, b, tm):
    t = x2d.shape[0]
    row_blk = lambda n: pl.BlockSpec((tm, n), lambda i: (i, 0))
    full = lambda a: pl.BlockSpec(a.shape, lambda i: (0, 0))
    return pl.pallas_call(
        _outproj_kernel,
        grid=(t // tm,),
        in_specs=[row_blk(ATT_WIDTH), row_blk(LSTM_WIDTH), row_blk(D_MODEL), full(w_out), full(g), full(b)],
        out_specs=row_blk(D_MODEL),
        out_shape=jax.ShapeDtypeStruct((t, D_MODEL), F32),
        compiler_params=_params(("parallel",)),
        name="outproj_ln1",
    )(att, lstm, x2d, w_out, g, b)


HALO = 8


def _ffn_kernel(x_ref, xp_ref, xn_ref, wg_ref, wu_ref, cw_ref, cb_ref, wd_ref, g_ref, b_ref, o_ref,
                gs_ref, *, seq, tm):
    i = pl.program_id(0)
    xb = x_ref[...].astype(BF16)
    has_prev = ((i * tm) % seq != 0).astype(F32)
    has_next = (((i + 1) * tm) % seq != 0).astype(F32)
    halo = jnp.concatenate([xp_ref[...] * has_prev, xn_ref[...] * has_next], axis=0).astype(BF16)
    acc = None
    for j in range(len(FF_BOUNDS) - 1):
        cs = slice(FF_BOUNDS[j], FF_BOUNDS[j + 1])
        gs_ref[HALO:HALO + tm, cs] = jnp.dot(xb, wg_ref[:, cs], preferred_element_type=F32)
        gh = jnp.dot(halo, wg_ref[:, cs], preferred_element_type=F32)
        gs_ref[0:HALO, cs] = gh[0:HALO]
        gs_ref[HALO + tm:2 * HALO + tm, cs] = gh[HALO:2 * HALO]
        conv = (cb_ref[:, cs]
                + gs_ref[HALO - 1:HALO - 1 + tm, cs] * cw_ref[0:1, cs]
                + gs_ref[HALO:HALO + tm, cs] * cw_ref[1:2, cs]
                + gs_ref[HALO + 1:HALO + 1 + tm, cs] * cw_ref[2:3, cs])
        up = jnp.dot(xb, wu_ref[:, cs], preferred_element_type=F32)
        gelu = 0.5 * conv * (lax.erf(conv * (1.0 / math.sqrt(2.0))) + 1.0)
        hmid = (gelu * up).astype(BF16)
        part = jnp.dot(hmid, wd_ref[cs, :], preferred_element_type=F32)
        acc = part if acc is None else acc + part
    o_ref[...] = _layer_norm(ALPHA * x_ref[...] + acc, g_ref[...], b_ref[...])


def _ffn(x2d, w_g, w_u, conv_w, conv_b, w_d, g, b, seq, tm):
    t = x2d.shape[0]
    hb = tm // HALO
    last_hblk = t // HALO - 1
    kernel = functools.partial(_ffn_kernel, seq=seq, tm=tm)
    resident = lambda a: pl.BlockSpec(a.shape, lambda i: (0, 0), pipeline_mode=pl.Buffered(1))
    return pl.pallas_call(
        kernel,
        grid=(t // tm,),
        in_specs=[
            pl.BlockSpec((tm, D_MODEL), lambda i: (i, 0)),
            pl.BlockSpec((HALO, D_MODEL), lambda i: (jnp.maximum(i * hb - 1, 0), 0)),
            pl.BlockSpec((HALO, D_MODEL), lambda i: (jnp.minimum((i + 1) * hb, last_hblk), 0)),
            resident(w_g), resident(w_u), resident(conv_w), resident(conv_b), resident(w_d),
            resident(g), resident(b),
        ],
        out_specs=pl.BlockSpec((tm, D_MODEL), lambda i: (i, 0)),
        out_shape=jax.ShapeDtypeStruct((t, D_MODEL), F32),
        scratch_shapes=[
            pltpu.VMEM((tm + 2 * HALO, D_FF), F32),
        ],
        compiler_params=_params(("parallel",)),
        name="conv_ffn_ln2",
    )(x2d, x2d, x2d, w_g, w_u, conv_w, conv_b, w_d, g, b)


def _prep_layer(l, w_in, gate_bias, lam_q1, lam_k1, lam_q2, lam_k2, att_norm_g, lstm_norm_g, w_out,
                ln1_g, ln1_b, w_gu, conv_w, conv_b, w_down, ln2_g, ln2_b):
    wl = w_in[l]
    row = lambda a: a.reshape(1, -1).astype(F32)
    return dict(
        w_main=wl[:, :C_LG].astype(BF16),
        w_akt=wl[:, C_AK:C_AV].T.astype(BF16),
        w_kt=wl[:, C_LK:C_LV].T.astype(BF16),
        w_gt=wl[:, C_LG:C_END].T.astype(BF16),
        gbias=gate_bias[l].reshape(N_GATES, 1).astype(F32),
        lq1=row(lam_q1[l]), lk1=row(lam_k1[l]), lq2=row(lam_q2[l]), lk2=row(lam_k2[l]),
        att_g=row(att_norm_g[l]), lstm_g=row(lstm_norm_g[l]),
        w_out=w_out[l].astype(BF16), ln1_g=row(ln1_g[l]), ln1_b=row(ln1_b[l]),
        w_g=w_gu[l][:, :D_FF].astype(BF16), w_u=w_gu[l][:, D_FF:].astype(BF16),
        conv_w=conv_w[l].astype(F32), conv_b=row(conv_b[l]),
        w_d=w_down[l].astype(BF16), ln2_g=row(ln2_g[l]), ln2_b=row(ln2_b[l]),
    )


def _trunk(x, layers, tm_proj=PROJ_BLOCK, tm=TOKEN_BLOCK):
    batch, seq, _ = x.shape
    x2d = x.reshape(batch * seq, D_MODEL)
    for l, p in enumerate(layers):
        lam_init = 0.8 - 0.6 * math.exp(-0.3 * l)
        aq, akt, av, lq, lkt, lv, lo, lgt = _inproj(x2d, p["w_main"], p["w_akt"], p["w_kt"], p["w_gt"],
                                                    p["gbias"], tm_proj)
        att = _attention(aq, akt, av, p["lq1"], p["lk1"], p["lq2"], p["lk2"], p["att_g"], batch, seq, lam_init)
        lstm = _mlstm(lq, lkt, lv, lo, lgt, p["lstm_g"], batch, seq)
        x2d = _outproj(att, lstm, x2d, p["w_out"], p["ln1_g"], p["ln1_b"], tm_proj)
        x2d = _ffn(x2d, p["w_g"], p["w_u"], p["conv_w"], p["conv_b"], p["w_d"], p["ln2_g"], p["ln2_b"], seq, tm)
    return x2d.reshape(batch, seq, D_MODEL)


def kernel(x_prompt, x_sample, w_in, gate_bias, lam_q1, lam_k1, lam_q2, lam_k2, att_norm_g, lstm_norm_g,
           w_out, ln1_g, ln1_b, w_gu, conv_w, conv_b, w_down, ln2_g, ln2_b):
    layers = [_prep_layer(l, w_in, gate_bias, lam_q1, lam_k1, lam_q2, lam_k2, att_norm_g, lstm_norm_g,
                          w_out, ln1_g, ln1_b, w_gu, conv_w, conv_b, w_down, ln2_g, ln2_b)
              for l in range(DEPTH)]
    return (_trunk(x_prompt, layers), _trunk(x_sample, layers))
```

```python
import functools
import math

import jax
import jax.numpy as jnp
from jax import lax
from jax.experimental import pallas as pl
from jax.experimental.pallas import tpu as pltpu

F32 = jnp.float32
BF16 = jnp.bfloat16

D_MODEL = 1024
DEPTH = 2
HEADS = 4
QK_DIM = 64
V_DIM = 128
ATT_WIDTH = HEADS * V_DIM
LSTM_WIDTH = HEADS * V_DIM
N_GATES = 4 * HEADS
D_FF = 2816
CHUNK = 128
ALPHA = (2 * DEPTH) ** 0.25
EPS = 1e-5
QK_SCALE = QK_DIM ** -0.5

C_AQ, C_AK, C_AV, C_LQ, C_LK, C_LV, C_LO, C_LG, C_END = 0, 512, 1024, 1536, 1792, 2048, 2560, 3072, 3088

LANES = 128
MXU_TILE = 256
VMEM_LIMIT = 56 * 1024 * 1024

PROJ_BLOCK = 1024
TOKEN_BLOCK = 512
FF_BOUNDS = (0, 6 * MXU_TILE, D_FF)
K_TILE = MXU_TILE
SCORE_BUF_BYTES = 16 * 1024 * 1024


def _attn_sub_blocks(seq):
    return min(SCORE_BUF_BYTES // (2 * K_TILE * seq * 4), seq // K_TILE)

NT_DIMS = (((1,), (1,)), ((), ()))
TN_DIMS = (((0,), (0,)), ((), ()))


def _params(sem):
    return pltpu.CompilerParams(dimension_semantics=sem, vmem_limit_bytes=VMEM_LIMIT)


def _layer_norm(z, g, b):
    mu = jnp.mean(z, axis=-1, keepdims=True)
    zc = z - mu
    var = jnp.mean(zc * zc, axis=-1, keepdims=True)
    return zc * lax.rsqrt(var + EPS) * g + b


def _inproj_kernel(x_ref, w_ref, wakt_ref, wkt_ref, wgt_ref, gb_ref,
                   aq_ref, akt_ref, av_ref, lq_ref, lkt_ref, lv_ref, lo_ref, lgt_ref):
    xb = x_ref[...].astype(BF16)

    def seg(a, b):
        return jnp.dot(xb, w_ref[:, a:b], preferred_element_type=F32)

    aq_ref[...] = (seg(C_AQ, C_AK) * QK_SCALE).astype(BF16)
    akt_ref[...] = lax.dot_general(wakt_ref[...], xb, NT_DIMS, preferred_element_type=F32).astype(BF16)
    av_ref[...] = seg(C_AV, C_LQ).astype(BF16)
    lq_ref[...] = seg(C_LQ, C_LK).astype(BF16)
    kt = lax.dot_general(wkt_ref[...], xb, NT_DIMS, preferred_element_type=F32)
    lkt_ref[...] = (kt * QK_SCALE).astype(BF16)
    lv_ref[...] = seg(C_LV, C_LO).astype(BF16)
    lo_ref[...] = seg(C_LO, C_LG).astype(BF16)
    g = lax.dot_general(wgt_ref[...], xb, NT_DIMS, preferred_element_type=F32) + gb_ref[...]
    row = lax.broadcasted_iota(jnp.int32, g.shape, 0)
    is_forget = ((row >> 2) & 1) == 1
    log_sig = jnp.minimum(g, 0.0) - jnp.log1p(jnp.exp(-jnp.abs(g)))
    lgt_ref[...] = jnp.where(is_forget, log_sig, g)


def _inproj(x2d, w_main, w_akt, w_kt, w_gt, gbias, tm):
    t = x2d.shape[0]
    grid = (t // tm,)
    row_blk = lambda n: pl.BlockSpec((tm, n), lambda i: (i, 0))
    full = lambda a: pl.BlockSpec(a.shape, lambda i: (0, 0))
    out_shapes = (
        jax.ShapeDtypeStruct((t, 512), BF16),
        jax.ShapeDtypeStruct((512, t), BF16),
        jax.ShapeDtypeStruct((t, ATT_WIDTH), BF16),
        jax.ShapeDtypeStruct((t, 256), BF16),
        jax.ShapeDtypeStruct((256, t), BF16),
        jax.ShapeDtypeStruct((t, LSTM_WIDTH), BF16),
        jax.ShapeDtypeStruct((t, LSTM_WIDTH), BF16),
        jax.ShapeDtypeStruct((N_GATES, t), F32),
    )
    out_specs = (
        row_blk(512), pl.BlockSpec((512, tm), lambda i: (0, i)), row_blk(ATT_WIDTH), row_blk(256),
        pl.BlockSpec((256, tm), lambda i: (0, i)),
        row_blk(LSTM_WIDTH), row_blk(LSTM_WIDTH),
        pl.BlockSpec((N_GATES, tm), lambda i: (0, i)),
    )
    return pl.pallas_call(
        _inproj_kernel,
        grid=grid,
        in_specs=[row_blk(D_MODEL), full(w_main), full(w_akt), full(w_kt), full(w_gt), full(gbias)],
        out_specs=out_specs,
        out_shape=out_shapes,
        compiler_params=_params(("parallel",)),
        name="inproj",
    )(x2d, w_main, w_akt, w_kt, w_gt, gbias)


def _alibi_slope(h):
    return jnp.where(h == 0, 0.25, jnp.where(h == 1, 0.0625, jnp.where(h == 2, 0.015625, 0.00390625))).astype(F32)


def _attn_kernel(q_ref, k_ref, v_ref, lq1_ref, lk1_ref, lq2_ref, lk2_ref, g_ref, o_ref,
                 kcat_ref, vaug_ref, qv_ref, sa_ref, sb_ref, *, seq, n_blocks, lam_init):
    SUB = _attn_sub_blocks(seq)
    Q_BLOCK = SUB * K_TILE
    n_tiles = seq // K_TILE
    nq = seq // Q_BLOCK
    step = pl.program_id(0)
    blk = jnp.minimum(step, n_blocks - 1)
    head_idx = blk // nq
    h = head_idx % HEADS
    qi = blk % nq
    prev_head_idx = jnp.maximum(step - 1, 0) // nq
    slope = _alibi_slope(h)

    @pl.when(step == 0)
    def _init_constants():
        sb_ref[...] = jnp.zeros_like(sb_ref)
        frow = lax.broadcasted_iota(jnp.int32, (LANES, seq), 0)
        j = lax.broadcasted_iota(jnp.int32, (LANES, seq), 1)
        kpos = jnp.where(frow < 2, 1.0,
                         jnp.where(frow == 2, (j >> 6).astype(F32),
                                   jnp.where(frow == 3, (j & 63).astype(F32), 0.0))).astype(BF16)
        zeros = jnp.zeros((QK_DIM, K_TILE), BF16)
        for t in range(n_tiles):
            cols = slice(t * K_TILE, (t + 1) * K_TILE)
            kcat_ref[t, QK_DIM:LANES, 0:K_TILE] = zeros
            kcat_ref[t, 0:QK_DIM, K_TILE:2 * K_TILE] = zeros
            kcat_ref[t, LANES:2 * LANES, 0:K_TILE] = kpos[:, cols]
            kcat_ref[t, LANES:2 * LANES, K_TILE:2 * K_TILE] = kpos[:, cols]
        for slot in range(2):
            vaug_ref[slot, :, V_DIM:2 * V_DIM] = jnp.ones((seq, V_DIM), BF16)

    @pl.when((qi == 0) & (step < n_blocks))
    def _load_keys():
        for t in range(n_tiles):
            cols = slice(t * K_TILE, (t + 1) * K_TILE)
            kcat_ref[t, 0:QK_DIM, 0:K_TILE] = k_ref[0:QK_DIM, cols]
            kcat_ref[t, QK_DIM:LANES, K_TILE:2 * K_TILE] = k_ref[QK_DIM:LANES, cols]
        vaug_ref[head_idx % 2, :, 0:V_DIM] = v_ref[...]

    def block_step(s_new, s_old):
        q = q_ref[...]
        lane = lax.broadcasted_iota(jnp.int32, (Q_BLOCK, LANES), 1)
        row = lax.broadcasted_iota(jnp.int32, (Q_BLOCK, LANES), 0)
        i = qi * Q_BLOCK + row
        feat = slope * jnp.where(lane == 0, -64.0 * (i >> 6).astype(F32),
                                 jnp.where(lane == 1, -(i & 63).astype(F32),
                                           jnp.where(lane == 2, 64.0, jnp.where(lane == 3, 1.0, 0.0))))
        sub = row // K_TILE
        for var in range(SUB + 2):
            if var == 0:
                coef = 1.0
            elif var == SUB + 1:
                coef = -1.0
            else:
                coef = jnp.where(sub > var - 1, 1.0, jnp.where(sub == var - 1, 0.0, -1.0))
            qv_ref[var, :, 0:LANES] = q
            qv_ref[var, :, LANES:2 * LANES] = (coef * feat).astype(BF16)
        for t in range(n_tiles):
            sel = jnp.clip(t - SUB * qi + 1, 0, SUB + 1)
            s = jnp.dot(qv_ref[sel], kcat_ref[t], preferred_element_type=F32)
            s_new[0, t] = s[:, 0:K_TILE]
            s_new[1, t] = s[:, K_TILE:2 * K_TILE]
        r = lax.broadcasted_iota(jnp.int32, (K_TILE, K_TILE), 0)
        c = lax.broadcasted_iota(jnp.int32, (K_TILE, K_TILE), 1)
        diag_bias = -slope * jnp.abs(r - c).astype(F32)
        for sb in range(SUB):
            rows = slice(sb * K_TILE, (sb + 1) * K_TILE)
            for m in range(2):
                s_new[m, SUB * qi + sb, rows, :] = s_new[m, SUB * qi + sb, rows, :] + diag_bias

        vslot = prev_head_idx % 2
        row_max = []
        for m in range(2):
            mx = s_old[m, 0]
            for t in range(1, n_tiles):
                mx = jnp.maximum(mx, s_old[m, t])
            row_max.append(jnp.broadcast_to(jnp.max(mx, axis=1, keepdims=True), (Q_BLOCK, K_TILE)))
        acc = jnp.zeros((2 * Q_BLOCK, 2 * V_DIM), F32)
        for t in range(n_tiles):
            p = jnp.concatenate([jnp.exp(s_old[m, t] - row_max[m]).astype(BF16) for m in range(2)], axis=0)
            acc = acc + jnp.dot(p, vaug_ref[vslot, t * K_TILE:(t + 1) * K_TILE, :],
                                preferred_element_type=F32)
        outs = [acc[m * Q_BLOCK:(m + 1) * Q_BLOCK, 0:V_DIM] / acc[m * Q_BLOCK:(m + 1) * Q_BLOCK, V_DIM:2 * V_DIM]
                for m in range(2)]
        lam = (jnp.exp(jnp.sum(lq1_ref[...] * lk1_ref[...], keepdims=True))
               - jnp.exp(jnp.sum(lq2_ref[...] * lk2_ref[...], keepdims=True)) + lam_init)
        att = outs[0] - lam * outs[1]
        att = att * lax.rsqrt(jnp.mean(att * att, axis=-1, keepdims=True) + EPS)
        o_ref[...] = (att * g_ref[...] * (1.0 - lam_init)).astype(o_ref.dtype)

    @pl.when(step % 2 == 0)
    def _even():
        block_step(sa_ref, sb_ref)

    @pl.when(step % 2 == 1)
    def _odd():
        block_step(sb_ref, sa_ref)


def _attention(aq, akt, av, lq1, lk1, lq2, lk2, att_g, batch, seq, lam_init):
    t = batch * seq
    SUB = _attn_sub_blocks(seq)
    Q_BLOCK = SUB * K_TILE
    nq = seq // Q_BLOCK
    n_tiles = seq // K_TILE
    n_blocks = batch * HEADS * nq

    def cur(i):
        blk = jnp.minimum(i, n_blocks - 1)
        return blk // (HEADS * nq), (blk // nq) % HEADS, blk % nq

    def prev(i):
        blk = jnp.maximum(i - 1, 0)
        return blk // (HEADS * nq), (blk // nq) % HEADS, blk % nq

    def q_map(i):
        b, h, q = cur(i)
        return b * nq + q, h

    def kt_map(i):
        b, h, _ = cur(i)
        return h, b

    def v_map(i):
        b, h, _ = cur(i)
        return b, h

    def out_map(i):
        b, h, q = prev(i)
        return b * nq + q, h

    small = pl.BlockSpec((1, QK_DIM), lambda i: (0, 0))
    kernel = functools.partial(_attn_kernel, seq=seq, n_blocks=n_blocks, lam_init=lam_init)
    score_buf = pltpu.VMEM((2, n_tiles, Q_BLOCK, K_TILE), F32)
    return pl.pallas_call(
        kernel,
        grid=(n_blocks + 1,),
        in_specs=[
            pl.BlockSpec((Q_BLOCK, LANES), q_map),
            pl.BlockSpec((LANES, seq), kt_map),
            pl.BlockSpec((seq, V_DIM), v_map),
            small, small, small, small,
            pl.BlockSpec((1, V_DIM), lambda i: (0, prev(i)[1])),
        ],
        out_specs=pl.BlockSpec((Q_BLOCK, V_DIM), out_map),
        out_shape=jax.ShapeDtypeStruct((t, ATT_WIDTH), BF16),
        scratch_shapes=[
            pltpu.VMEM((n_tiles, 2 * LANES, 2 * K_TILE), BF16),
            pltpu.VMEM((2, seq, 2 * V_DIM), BF16),
            pltpu.VMEM((SUB + 2, Q_BLOCK, 2 * LANES), BF16),
            score_buf, score_buf,
        ],
        compiler_params=_params(("arbitrary",)),
        name="diff_attention",
    )(aq, akt, av, lq1, lk1, lq2, lk2, att_g)


def _split3(x):
    hi = x.astype(BF16).astype(F32)
    r1 = x - hi
    mid = r1.astype(BF16).astype(F32)
    lo = (r1 - mid).astype(BF16).astype(F32)
    return hi, mid, lo


def _chunk_scan(x, pos, combine, fill, reverse, seq):
    k = 1
    while k < CHUNK:
        if reverse:
            shifted = pltpu.roll(x, seq - k, 1)
            ok = pos < CHUNK - k
        else:
            shifted = pltpu.roll(x, k, 1)
            ok = pos >= k
        x = combine(x, jnp.where(ok, shifted, fill))
        k *= 2
    return x


def _mlstm_kernel(q_ref, kt_ref, v_ref, o_ref_in, gt_ref, g_ref, out_ref,
                  u_ref, bs_ref, cm_ref, hacc_ref, kz_ref, c_ref, cext_ref, lhs_ref, rhs_ref, *, seq):
    L = CHUNK
    nc = seq // L

    gts = gt_ref[...]
    row = lax.broadcasted_iota(jnp.int32, gts.shape, 0)
    pos = lax.broadcasted_iota(jnp.int32, gts.shape, 1) & (L - 1)
    psum = _chunk_scan(gts, pos, jnp.add, 0.0, False, seq)
    ssum = _chunk_scan(gts, pos, jnp.add, 0.0, True, seq)
    bsum = jnp.where(row < 2 * HEADS, psum, ssum)
    u = gts - pltpu.roll(bsum, N_GATES - HEADS, 0)
    cmax = jnp.where(row < 2 * HEADS,
                     _chunk_scan(u, pos, jnp.maximum, -jnp.inf, False, seq),
                     _chunk_scan(u, pos, jnp.maximum, -jnp.inf, True, seq))
    u_ref[...] = u
    bs_ref[...] = bsum
    cm_ref[...] = cmax

    hacc_ref[...] = jnp.zeros_like(hacc_ref)
    kz_ref[...] = jnp.zeros_like(kz_ref)
    c_ref[...] = jnp.zeros_like(c_ref)
    cext_ref[...] = jnp.zeros_like(cext_ref)
    lrow = lax.broadcasted_iota(jnp.int32, (16, L), 0)
    lhs_const = jnp.where(lrow < 3, 1.0, 0.0).astype(F32)
    rrow = lax.broadcasted_iota(jnp.int32, (16, 3 * L), 0)
    rcol = lax.broadcasted_iota(jnp.int32, (16, 3 * L), 1) >> 7
    rhs_const = jnp.where(((rrow >= 3) & (rrow < 6)) | ((rrow >= 6) & (rrow < 9) & (rcol == 2)),
                          1.0, 0.0).astype(F32)
    for n in range(2 * HEADS):
        lhs_ref[n] = lhs_const
        rhs_ref[n] = rhs_const

    jj = lax.broadcasted_iota(jnp.int32, (L, L), 0)
    ss = lax.broadcasted_iota(jnp.int32, (L, L), 1)
    masks = (ss <= jj, ss >= jj)
    ones_v = jnp.ones((L, V_DIM), BF16)

    def chunk_step(d, h, c, m_prev):
        n = d * HEADS + h
        c0 = pl.multiple_of(c * L, L)
        cols = pl.ds(c0, L)
        hrows = slice(h * QK_DIM, (h + 1) * QK_DIM)
        hcols = slice(h * V_DIM, (h + 1) * V_DIM)
        u_row = u_ref[8 * d + h:8 * d + h + 1, cols]
        b_row = bs_ref[8 * d + HEADS + h:8 * d + HEADS + h + 1, cols]
        cm_row = cm_ref[8 * d + h:8 * d + h + 1, cols]
        umax = jnp.max(u_row, axis=1, keepdims=True)
        g_tot = b_row[:, L - 1:L] if d == 0 else b_row[:, 0:1]
        mx_row = jnp.maximum(m_prev, cm_row)

        for r0, piece in ((3, -mx_row), (6, -b_row)):
            for k, part in enumerate(_split3(piece)):
                lhs_ref[n, r0 + k:r0 + k + 1, :] = part
        for k, part in enumerate(_split3(u_row)):
            rhs_ref[n, k:k + 1, 0:L] = part
        for k, part in enumerate(_split3(m_prev)):
            rhs_ref[n, k:k + 1, L:2 * L] = jnp.broadcast_to(part, (1, L))
        outer = lax.dot_general(lhs_ref[n].astype(BF16), rhs_ref[n].astype(BF16), TN_DIMS,
                                preferred_element_type=F32)
        e = jnp.exp(outer)
        p = jnp.where(masks[d], e[:, 0:L], 0.0)
        e_inter = e[:, L:2 * L]
        clamp = e[:, 2 * L:3 * L]

        q = q_ref[cols, :]
        kh = kt_ref[hrows, cols]
        kz_ref[n, hrows, :] = kh
        qk = jnp.dot(q, kz_ref[n], preferred_element_type=F32)
        sqk = (qk * p).astype(BF16)
        vaug = jnp.concatenate([v_ref[cols, hcols], ones_v], axis=1)
        ke = (kh.astype(F32) * jnp.exp(u_row - umax)).astype(BF16)
        both = jnp.dot(jnp.concatenate([sqk, ke], axis=0), vaug, preferred_element_type=F32)
        intra = both[0:L]
        c_loc = both[L:L + QK_DIM]
        inter = jnp.dot(q, cext_ref[n], preferred_element_type=F32)
        nd = intra + jnp.concatenate([e_inter, e_inter], axis=1) * inter
        hout = nd[:, 0:V_DIM] / jnp.maximum(jnp.abs(nd[:, V_DIM:2 * V_DIM]), clamp)
        hacc_ref[cols, hcols] = hacc_ref[cols, hcols] + hout

        m_loc = g_tot + umax
        m_new = jnp.maximum(g_tot + m_prev, m_loc)
        a = jnp.exp(g_tot + m_prev - m_new)
        bb = jnp.exp(m_loc - m_new)
        c_new = a * c_ref[n] + bb * c_loc
        c_ref[n] = c_new
        cext_ref[n, hrows, :] = c_new.astype(BF16)
        return m_new

    def body(it, carry):
        new = []
        for d in range(2):
            c = it if d == 0 else nc - 1 - it
            for h in range(HEADS):
                new.append(chunk_step(d, h, c, carry[d * HEADS + h]))
        return tuple(new)

    zero = jnp.zeros((1, 1), F32)
    lax.fori_loop(0, nc, body, (zero,) * (2 * HEADS), unroll=4)

    for h in range(HEADS):
        hcols = slice(h * V_DIM, (h + 1) * V_DIM)
        hs = hacc_ref[:, hcols]
        hn = hs * lax.rsqrt(jnp.mean(hs * hs, axis=-1, keepdims=True) + EPS) * g_ref[:, hcols]
        out_ref[:, hcols] = (jax.nn.sigmoid(o_ref_in[:, hcols].astype(F32)) * hn).astype(out_ref.dtype)


def _mlstm(lq, lkt, lv, lo, lgt, lstm_g, batch, seq):
    t = batch * seq
    kernel = functools.partial(_mlstm_kernel, seq=seq)
    L = CHUNK
    n_rec = 2 * HEADS
    gate_rows = pltpu.VMEM((N_GATES, seq), F32)
    return pl.pallas_call(
        kernel,
        grid=(batch,),
        in_specs=[
            pl.BlockSpec((seq, 256), lambda b: (b, 0)),
            pl.BlockSpec((256, seq), lambda b: (0, b)),
            pl.BlockSpec((seq, LSTM_WIDTH), lambda b: (b, 0)),
            pl.BlockSpec((seq, LSTM_WIDTH), lambda b: (b, 0)),
            pl.BlockSpec((N_GATES, seq), lambda b: (0, b)),
            pl.BlockSpec((1, LSTM_WIDTH), lambda b: (0, 0)),
        ],
        out_specs=pl.BlockSpec((seq, LSTM_WIDTH), lambda b: (b, 0)),
        out_shape=jax.ShapeDtypeStruct((t, LSTM_WIDTH), BF16),
        scratch_shapes=[
            gate_rows,
            gate_rows,
            gate_rows,
            pltpu.VMEM((seq, LSTM_WIDTH), F32),
            pltpu.VMEM((n_rec, 256, L), BF16),
            pltpu.VMEM((n_rec, QK_DIM, 2 * V_DIM), F32),
            pltpu.VMEM((n_rec, 256, 2 * V_DIM), BF16),
            pltpu.VMEM((n_rec, 16, L), F32),
            pltpu.VMEM((n_rec, 16, 3 * L), F32),
        ],
        compiler_params=_params(("parallel",)),
        name="mlstm",
    )(lq, lkt, lv, lo, lgt, lstm_g)


def _outproj_kernel(att_ref, lstm_ref, x_ref, w_ref, g_ref, b_ref, o_ref):
    y = jnp.dot(att_ref[...], w_ref[0:ATT_WIDTH, :], preferred_element_type=F32)
    y = y + jnp.dot(lstm_ref[...], w_ref[ATT_WIDTH:ATT_WIDTH + LSTM_WIDTH, :], preferred_element_type=F32)
    o_ref[...] = _layer_norm(ALPHA * x_ref[...] + y, g_ref[...], b_ref[...])


def _outproj(att, lstm, x2d, w_out, g, b, tm):
    t = x2d.shape[0]
    row_blk = lambda n: pl.BlockSpec((tm, n), lambda i: (i, 0))
    full = lambda a: pl.BlockSpec(a.shape, lambda i: (0, 0))
    return pl.pallas_call(
        _outproj_kernel,
        grid=(t // tm,),
        in_specs=[row_blk(ATT_WIDTH), row_blk(LSTM_WIDTH), row_blk(D_MODEL), full(w_out), full(g), full(b)],
        out_specs=row_blk(D_MODEL),
        out_shape=jax.ShapeDtypeStruct((t, D_MODEL), F32),
        compiler_params=_params(("parallel",)),
        name="outproj_ln1",
    )(att, lstm, x2d, w_out, g, b)


HALO = 8


def _ffn_kernel(x_ref, xp_ref, xn_ref, wg_ref, wu_ref, cw_ref, cb_ref, wd_ref, g_ref, b_ref, o_ref,
                gs_ref, *, seq, tm):
    i = pl.program_id(0)
    xb = x_ref[...].astype(BF16)
    has_prev = ((i * tm) % seq != 0).astype(F32)
    has_next = (((i + 1) * tm) % seq != 0).astype(F32)
    halo = jnp.concatenate([xp_ref[...] * has_prev, xn_ref[...] * has_next], axis=0).astype(BF16)
    acc = None
    for j in range(len(FF_BOUNDS) - 1):
        cs = slice(FF_BOUNDS[j], FF_BOUNDS[j + 1])
        gs_ref[HALO:HALO + tm, cs] = jnp.dot(xb, wg_ref[:, cs], preferred_element_type=F32)
        gh = jnp.dot(halo, wg_ref[:, cs], preferred_element_type=F32)
        gs_ref[0:HALO, cs] = gh[0:HALO]
        gs_ref[HALO + tm:2 * HALO + tm, cs] = gh[HALO:2 * HALO]
        conv = (cb_ref[:, cs]
                + gs_ref[HALO - 1:HALO - 1 + tm, cs] * cw_ref[0:1, cs]
                + gs_ref[HALO:HALO + tm, cs] * cw_ref[1:2, cs]
                + gs_ref[HALO + 1:HALO + 1 + tm, cs] * cw_ref[2:3, cs])
        up = jnp.dot(xb, wu_ref[:, cs], preferred_element_type=F32)
        gelu = 0.5 * conv * (lax.erf(conv * (1.0 / math.sqrt(2.0))) + 1.0)
        hmid = (gelu * up).astype(BF16)
        part = jnp.dot(hmid, wd_ref[cs, :], preferred_element_type=F32)
        acc = part if acc is None else acc + part
    o_ref[...] = _layer_norm(ALPHA * x_ref[...] + acc, g_ref[...], b_ref[...])


def _ffn(x2d, w_g, w_u, conv_w, conv_b, w_d, g, b, seq, tm):
    t = x2d.shape[0]
    hb = tm // HALO
    last_hblk = t // HALO - 1
    kernel = functools.partial(_ffn_kernel, seq=seq, tm=tm)
    resident = lambda a: pl.BlockSpec(a.shape, lambda i: (0, 0), pipeline_mode=pl.Buffered(1))
    return pl.pallas_call(
        kernel,
        grid=(t // tm,),
        in_specs=[
            pl.BlockSpec((tm, D_MODEL), lambda i: (i, 0)),
            pl.BlockSpec((HALO, D_MODEL), lambda i: (jnp.maximum(i * hb - 1, 0), 0)),
            pl.BlockSpec((HALO, D_MODEL), lambda i: (jnp.minimum((i + 1) * hb, last_hblk), 0)),
            resident(w_g), resident(w_u), resident(conv_w), resident(conv_b), resident(w_d),
            resident(g), resident(b),
        ],
        out_specs=pl.BlockSpec((tm, D_MODEL), lambda i: (i, 0)),
        out_shape=jax.ShapeDtypeStruct((t, D_MODEL), F32),
        scratch_shapes=[
            pltpu.VMEM((tm + 2 * HALO, D_FF), F32),
        ],
        compiler_params=_params(("parallel",)),
        name="conv_ffn_ln2",
    )(x2d, x2d, x2d, w_g, w_u, conv_w, conv_b, w_d, g, b)


def _prep_layer(l, w_in, gate_bias, lam_q1, lam_k1, lam_q2, lam_k2, att_norm_g, lstm_norm_g, w_out,
                ln1_g, ln1_b, w_gu, conv_w, conv_b, w_down, ln2_g, ln2_b):
    wl = w_in[l]
    row = lambda a: a.reshape(1, -1).astype(F32)
    return dict(
        w_main=wl[:, :C_LG].astype(BF16),
        w_akt=wl[:, C_AK:C_AV].T.astype(BF16),
        w_kt=wl[:, C_LK:C_LV].T.astype(BF16),
        w_gt=wl[:, C_LG:C_END].T.astype(BF16),
        gbias=gate_bias[l].reshape(N_GATES, 1).astype(F32),
        lq1=row(lam_q1[l]), lk1=row(lam_k1[l]), lq2=row(lam_q2[l]), lk2=row(lam_k2[l]),
        att_g=row(att_norm_g[l]), lstm_g=row(lstm_norm_g[l]),
        w_out=w_out[l].astype(BF16), ln1_g=row(ln1_g[l]), ln1_b=row(ln1_b[l]),
        w_g=w_gu[l][:, :D_FF].astype(BF16), w_u=w_gu[l][:, D_FF:].astype(BF16),
        conv_w=conv_w[l].astype(F32), conv_b=row(conv_b[l]),
        w_d=w_down[l].astype(BF16), ln2_g=row(ln2_g[l]), ln2_b=row(ln2_b[l]),
    )


def _trunk(x, layers, tm_proj=PROJ_BLOCK, tm=TOKEN_BLOCK):
    batch, seq, _ = x.shape
    x2d = x.reshape(batch * seq, D_MODEL)
    for l, p in enumerate(layers):
        lam_init = 0.8 - 0.6 * math.exp(-0.3 * l)
        aq, akt, av, lq, lkt, lv, lo, lgt = _inproj(x2d, p["w_main"], p["w_akt"], p["w_kt"], p["w_gt"],
                                                    p["gbias"], tm_proj)
        att = _attention(aq, akt, av, p["lq1"], p["lk1"], p["lq2"], p["lk2"], p["att_g"], batch, seq, lam_init)
        lstm = _mlstm(lq, lkt, lv, lo, lgt, p["lstm_g"], batch, seq)
        x2d = _outproj(att, lstm, x2d, p["w_out"], p["ln1_g"], p["ln1_b"], tm_proj)
        x2d = _ffn(x2d, p["w_g"], p["w_u"], p["conv_w"], p["conv_b"], p["w_d"], p["ln2_g"], p["ln2_b"], seq, tm)
    return x2d.reshape(batch, seq, D_MODEL)


def kernel(x_prompt, x_sample, w_in, gate_bias, lam_q1, lam_k1, lam_q2, lam_k2, att_norm_g, lstm_norm_g,
           w_out, ln1_g, ln1_b, w_gu, conv_w, conv_b, w_down, ln2_g, ln2_b):
    layers = [_prep_layer(l, w_in, gate_bias, lam_q1, lam_k1, lam_q2, lam_k2, att_norm_g, lstm_norm_g,
                          w_out, ln1_g, ln1_b, w_gu, conv_w, conv_b, w_down, ln2_g, ln2_b)
              for l in range(DEPTH)]
    return (_trunk(x_prompt, layers), _trunk(x_sample, layers))
```

```python
import functools
import math

import jax
import jax.numpy as jnp
from jax import lax
from jax.experimental import pallas as pl
from jax.experimental.pallas import tpu as pltpu

F32 = jnp.float32
BF16 = jnp.bfloat16

D_MODEL = 1024
DEPTH = 2
HEADS = 4
QK_DIM = 64
V_DIM = 128
ATT_WIDTH = HEADS * V_DIM
LSTM_WIDTH = HEADS * V_DIM
N_GATES = 4 * HEADS
D_FF = 2816
CHUNK = 128
ALPHA = (2 * DEPTH) ** 0.25
EPS = 1e-5
QK_SCALE = QK_DIM ** -0.5

C_AQ, C_AK, C_AV, C_LQ, C_LK, C_LV, C_LO, C_LG, C_END = 0, 512, 1024, 1536, 1792, 2048, 2560, 3072, 3088

LANES = 128
MXU_TILE = 256
VMEM_LIMIT = 56 * 1024 * 1024

PROJ_BLOCK = 1024
TOKEN_BLOCK = 512
FF_BOUNDS = (0, 6 * MXU_TILE, D_FF)
K_TILE = MXU_TILE
SCORE_BUF_BYTES = 16 * 1024 * 1024


def _attn_sub_blocks(seq):
    return min(SCORE_BUF_BYTES // (2 * K_TILE * seq * 4), seq // K_TILE)

NT_DIMS = (((1,), (1,)), ((), ()))
TN_DIMS = (((0,), (0,)), ((), ()))


def _params(sem):
    return pltpu.CompilerParams(dimension_semantics=sem, vmem_limit_bytes=VMEM_LIMIT)


def _layer_norm(z, g, b):
    mu = jnp.mean(z, axis=-1, keepdims=True)
    zc = z - mu
    var = jnp.mean(zc * zc, axis=-1, keepdims=True)
    return zc * lax.rsqrt(var + EPS) * g + b


def _inproj_kernel(x_ref, w_ref, wakt_ref, wkt_ref, wgt_ref, gb_ref,
                   aq_ref, akt_ref, av_ref, lq_ref, lkt_ref, lv_ref, lo_ref, lgt_ref):
    xb = x_ref[...].astype(BF16)

    def seg(a, b):
        return jnp.dot(xb, w_ref[:, a:b], preferred_element_type=F32)

    aq_ref[...] = (seg(C_AQ, C_AK) * QK_SCALE).astype(BF16)
    akt_ref[...] = lax.dot_general(wakt_ref[...], xb, NT_DIMS, preferred_element_type=F32).astype(BF16)
    av_ref[...] = seg(C_AV, C_LQ).astype(BF16)
    lq_ref[...] = seg(C_LQ, C_LK).astype(BF16)
    kt = lax.dot_general(wkt_ref[...], xb, NT_DIMS, preferred_element_type=F32)
    lkt_ref[...] = (kt * QK_SCALE).astype(BF16)
    lv_ref[...] = seg(C_LV, C_LO).astype(BF16)
    lo_ref[...] = seg(C_LO, C_LG).astype(BF16)
    g = lax.dot_general(wgt_ref[...], xb, NT_DIMS, preferred_element_type=F32) + gb_ref[...]
    row = lax.broadcasted_iota(jnp.int32, g.shape, 0)
    is_forget = ((row >> 2) & 1) == 1
    log_sig = jnp.minimum(g, 0.0) - jnp.log1p(jnp.exp(-jnp.abs(g)))
    lgt_ref[...] = jnp.where(is_forget, log_sig, g)


def _inproj(x2d, w_main, w_akt, w_kt, w_gt, gbias, tm):
    t = x2d.shape[0]
    grid = (t // tm,)
    row_blk = lambda n: pl.BlockSpec((tm, n), lambda i: (i, 0))
    full = lambda a: pl.BlockSpec(a.shape, lambda i: (0, 0))
    out_shapes = (
        jax.ShapeDtypeStruct((t, 512), BF16),
        jax.ShapeDtypeStruct((512, t), BF16),
        jax.ShapeDtypeStruct((t, ATT_WIDTH), BF16),
        jax.ShapeDtypeStruct((t, 256), BF16),
        jax.ShapeDtypeStruct((256, t), BF16),
        jax.ShapeDtypeStruct((t, LSTM_WIDTH), BF16),
        jax.ShapeDtypeStruct((t, LSTM_WIDTH), BF16),
        jax.ShapeDtypeStruct((N_GATES, t), F32),
    )
    out_specs = (
        row_blk(512), pl.BlockSpec((512, tm), lambda i: (0, i)), row_blk(ATT_WIDTH), row_blk(256),
        pl.BlockSpec((256, tm), lambda i: (0, i)),
        row_blk(LSTM_WIDTH), row_blk(LSTM_WIDTH),
        pl.BlockSpec((N_GATES, tm), lambda i: (0, i)),
    )
    return pl.pallas_call(
        _inproj_kernel,
        grid=grid,
        in_specs=[row_blk(D_MODEL), full(w_main), full(w_akt), full(w_kt), full(w_gt), full(gbias)],
        out_specs=out_specs,
        out_shape=out_shapes,
        compiler_params=_params(("parallel",)),
        name="inproj",
    )(x2d, w_main, w_akt, w_kt, w_gt, gbias)


def _alibi_slope(h):
    return jnp.where(h == 0, 0.25, jnp.where(h == 1, 0.0625, jnp.where(h == 2, 0.015625, 0.00390625))).astype(F32)


def _attn_kernel(q_ref, k_ref, v_ref, lq1_ref, lk1_ref, lq2_ref, lk2_ref, g_ref, o_ref,
                 kcat_ref, vaug_ref, qv_ref, sa_ref, sb_ref, *, seq, n_blocks, lam_init):
    SUB = _attn_sub_blocks(seq)
    Q_BLOCK = SUB * K_TILE
    n_tiles = seq // K_TILE
    nq = seq // Q_BLOCK
    step = pl.program_id(0)
    blk = jnp.minimum(step, n_blocks - 1)
    head_idx = blk // nq
    h = head_idx % HEADS
    qi = blk % nq
    prev_head_idx = jnp.maximum(step - 1, 0) // nq
    slope = _alibi_slope(h)

    @pl.when(step == 0)
    def _init_constants():
        sb_ref[...] = jnp.zeros_like(sb_ref)
        frow = lax.broadcasted_iota(jnp.int32, (LANES, seq), 0)
        j = lax.broadcasted_iota(jnp.int32, (LANES, seq), 1)
        kpos = jnp.where(frow < 2, 1.0,
                         jnp.where(frow == 2, (j >> 6).astype(F32),
                                   jnp.where(frow == 3, (j & 63).astype(F32), 0.0))).astype(BF16)
        zeros = jnp.zeros((QK_DIM, K_TILE), BF16)
        for t in range(n_tiles):
            cols = slice(t * K_TILE, (t + 1) * K_TILE)
            kcat_ref[t, QK_DIM:LANES, 0:K_TILE] = zeros
            kcat_ref[t, 0:QK_DIM, K_TILE:2 * K_TILE] = zeros
            kcat_ref[t, LANES:2 * LANES, 0:K_TILE] = kpos[:, cols]
            kcat_ref[t, LANES:2 * LANES, K_TILE:2 * K_TILE] = kpos[:, cols]
        for slot in range(2):
            vaug_ref[slot, :, V_DIM:2 * V_DIM] = jnp.ones((seq, V_DIM), BF16)

    @pl.when((qi == 0) & (step < n_blocks))
    def _load_keys():
        for t in range(n_tiles):
            cols = slice(t * K_TILE, (t + 1) * K_TILE)
            kcat_ref[t, 0:QK_DIM, 0:K_TILE] = k_ref[0:QK_DIM, cols]
            kcat_ref[t, QK_DIM:LANES, K_TILE:2 * K_TILE] = k_ref[QK_DIM:LANES, cols]
        vaug_ref[head_idx % 2, :, 0:V_DIM] = v_ref[...]

    def block_step(s_new, s_old):
        q = q_ref[...]
        lane = lax.broadcasted_iota(jnp.int32, (Q_BLOCK, LANES), 1)
        row = lax.broadcasted_iota(jnp.int32, (Q_BLOCK, LANES), 0)
        i = qi * Q_BLOCK + row
        feat = slope * jnp.where(lane == 0, -64.0 * (i >> 6).astype(F32),
                                 jnp.where(lane == 1, -(i & 63).astype(F32),
                                           jnp.where(lane == 2, 64.0, jnp.where(lane == 3, 1.0, 0.0))))
        sub = row // K_TILE
        for var in range(SUB + 2):
            if var == 0:
                coef = 1.0
            elif var == SUB + 1:
                coef = -1.0
            else:
                coef = jnp.where(sub > var - 1, 1.0, jnp.where(sub == var - 1, 0.0, -1.0))
            qv_ref[var, :, 0:LANES] = q
            qv_ref[var, :, LANES:2 * LANES] = (coef * feat).astype(BF16)

        vslot = prev_head_idx % 2
        row_max = []
        for m in range(2):
            mx = s_old[m, 0]
            for t in range(1, n_tiles):
                mx = jnp.maximum(mx, s_old[m, t])
            row_max.append(jnp.broadcast_to(jnp.max(mx, axis=1, keepdims=True), (Q_BLOCK, K_TILE)))

        acc = jnp.zeros((2 * Q_BLOCK, 2 * V_DIM), F32)
        for t in range(n_tiles):
            p = jnp.concatenate([jnp.exp(s_old[m, t] - row_max[m]).astype(BF16) for m in range(2)], axis=0)
            acc = acc + jnp.dot(p, vaug_ref[vslot, t * K_TILE:(t + 1) * K_TILE, :],
                                preferred_element_type=F32)
            sel = jnp.clip(t - SUB * qi + 1, 0, SUB + 1)
            s = jnp.dot(qv_ref[sel], kcat_ref[t], preferred_element_type=F32)
            s_new[0, t] = s[:, 0:K_TILE]
            s_new[1, t] = s[:, K_TILE:2 * K_TILE]
        r = lax.broadcasted_iota(jnp.int32, (K_TILE, K_TILE), 0)
        c = lax.broadcasted_iota(jnp.int32, (K_TILE, K_TILE), 1)
        diag_bias = -slope * jnp.abs(r - c).astype(F32)
        for sb in range(SUB):
            rows = slice(sb * K_TILE, (sb + 1) * K_TILE)
            for m in range(2):
                s_new[m, SUB * qi + sb, rows, :] = s_new[m, SUB * qi + sb, rows, :] + diag_bias

        outs = [acc[m * Q_BLOCK:(m + 1) * Q_BLOCK, 0:V_DIM] / acc[m * Q_BLOCK:(m + 1) * Q_BLOCK, V_DIM:2 * V_DIM]
                for m in range(2)]
        lam = (jnp.exp(jnp.sum(lq1_ref[...] * lk1_ref[...], keepdims=True))
               - jnp.exp(jnp.sum(lq2_ref[...] * lk2_ref[...], keepdims=True)) + lam_init)
        att = outs[0] - lam * outs[1]
        att = att * lax.rsqrt(jnp.mean(att * att, axis=-1, keepdims=True) + EPS)
        o_ref[...] = (att * g_ref[...] * (1.0 - lam_init)).astype(o_ref.dtype)

    @pl.when(step % 2 == 0)
    def _even():
        block_step(sa_ref, sb_ref)

    @pl.when(step % 2 == 1)
    def _odd():
        block_step(sb_ref, sa_ref)


def _attention(aq, akt, av, lq1, lk1, lq2, lk2, att_g, batch, seq, lam_init):
    t = batch * seq
    SUB = _attn_sub_blocks(seq)
    Q_BLOCK = SUB * K_TILE
    nq = seq // Q_BLOCK
    n_tiles = seq // K_TILE
    n_blocks = batch * HEADS * nq

    def cur(i):
        blk = jnp.minimum(i, n_blocks - 1)
        return blk // (HEADS * nq), (blk // nq) % HEADS, blk % nq

    def prev(i):
        blk = jnp.maximum(i - 1, 0)
        return blk // (HEADS * nq), (blk // nq) % HEADS, blk % nq

    def q_map(i):
        b, h, q = cur(i)
        return b * nq + q, h

    def kt_map(i):
        b, h, _ = cur(i)
        return h, b

    def v_map(i):
        b, h, _ = cur(i)
        return b, h

    def out_map(i):
        b, h, q = prev(i)
        return b * nq + q, h

    small = pl.BlockSpec((1, QK_DIM), lambda i: (0, 0))
    kernel = functools.partial(_attn_kernel, seq=seq, n_blocks=n_blocks, lam_init=lam_init)
    score_buf = pltpu.VMEM((2, n_tiles, Q_BLOCK, K_TILE), F32)
    return pl.pallas_call(
        kernel,
        grid=(n_blocks + 1,),
        in_specs=[
            pl.BlockSpec((Q_BLOCK, LANES), q_map),
            pl.BlockSpec((LANES, seq), kt_map),
            pl.BlockSpec((seq, V_DIM), v_map),
            small, small, small, small,
            pl.BlockSpec((1, V_DIM), lambda i: (0, prev(i)[1])),
        ],
        out_specs=pl.BlockSpec((Q_BLOCK, V_DIM), out_map),
        out_shape=jax.ShapeDtypeStruct((t, ATT_WIDTH), BF16),
        scratch_shapes=[
            pltpu.VMEM((n_tiles, 2 * LANES, 2 * K_TILE), BF16),
            pltpu.VMEM((2, seq, 2 * V_DIM), BF16),
            pltpu.VMEM((SUB + 2, Q_BLOCK, 2 * LANES), BF16),
            score_buf, score_buf,
        ],
        compiler_params=_params(("arbitrary",)),
        name="diff_attention",
    )(aq, akt, av, lq1, lk1, lq2, lk2, att_g)


def _split3(x):
    hi = x.astype(BF16).astype(F32)
    r1 = x - hi
    mid = r1.astype(BF16).astype(F32)
    lo = (r1 - mid).astype(BF16).astype(F32)
    return hi, mid, lo


def _chunk_scan(x, pos, combine, fill, reverse, seq):
    k = 1
    while k < CHUNK:
        if reverse:
            shifted = pltpu.roll(x, seq - k, 1)
            ok = pos < CHUNK - k
        else:
            shifted = pltpu.roll(x, k, 1)
            ok = pos >= k
        x = combine(x, jnp.where(ok, shifted, fill))
        k *= 2
    return x


def _mlstm_kernel(q_ref, kt_ref, v_ref, o_ref_in, gt_ref, g_ref, out_ref,
                  u_ref, bs_ref, cm_ref, hacc_ref, kz_ref, c_ref, cext_ref, lhs_ref, rhs_ref, *, seq):
    L = CHUNK
    nc = seq // L

    gts = gt_ref[...]
    row = lax.broadcasted_iota(jnp.int32, gts.shape, 0)
    pos = lax.broadcasted_iota(jnp.int32, gts.shape, 1) & (L - 1)
    psum = _chunk_scan(gts, pos, jnp.add, 0.0, False, seq)
    ssum = _chunk_scan(gts, pos, jnp.add, 0.0, True, seq)
    bsum = jnp.where(row < 2 * HEADS, psum, ssum)
    u = gts - pltpu.roll(bsum, N_GATES - HEADS, 0)
    cmax = jnp.where(row < 2 * HEADS,
                     _chunk_scan(u, pos, jnp.maximum, -jnp.inf, False, seq),
                     _chunk_scan(u, pos, jnp.maximum, -jnp.inf, True, seq))
    u_ref[...] = u
    bs_ref[...] = bsum
    cm_ref[...] = cmax

    hacc_ref[...] = jnp.zeros_like(hacc_ref)
    kz_ref[...] = jnp.zeros_like(kz_ref)
    c_ref[...] = jnp.zeros_like(c_ref)
    cext_ref[...] = jnp.zeros_like(cext_ref)
    lrow = lax.broadcasted_iota(jnp.int32, (16, L), 0)
    lhs_const = jnp.where(lrow < 3, 1.0, 0.0).astype(F32)
    rrow = lax.broadcasted_iota(jnp.int32, (16, 3 * L), 0)
    rcol = lax.broadcasted_iota(jnp.int32, (16, 3 * L), 1) >> 7
    rhs_const = jnp.where(((rrow >= 3) & (rrow < 6)) | ((rrow >= 6) & (rrow < 9) & (rcol == 2)),
                          1.0, 0.0).astype(F32)
    for n in range(2 * HEADS):
        lhs_ref[n] = lhs_const
        rhs_ref[n] = rhs_const

    jj = lax.broadcasted_iota(jnp.int32, (L, L), 0)
    ss = lax.broadcasted_iota(jnp.int32, (L, L), 1)
    masks = (ss <= jj, ss >= jj)
    ones_v = jnp.ones((L, V_DIM), BF16)

    def chunk_step(d, h, c, m_prev):
        n = d * HEADS + h
        c0 = pl.multiple_of(c * L, L)
        cols = pl.ds(c0, L)
        hrows = slice(h * QK_DIM, (h + 1) * QK_DIM)
        hcols = slice(h * V_DIM, (h + 1) * V_DIM)
        u_row = u_ref[8 * d + h:8 * d + h + 1, cols]
        b_row = bs_ref[8 * d + HEADS + h:8 * d + HEADS + h + 1, cols]
        cm_row = cm_ref[8 * d + h:8 * d + h + 1, cols]
        umax = jnp.max(u_row, axis=1, keepdims=True)
        g_tot = b_row[:, L - 1:L] if d == 0 else b_row[:, 0:1]
        mx_row = jnp.maximum(m_prev, cm_row)

        for r0, piece in ((3, -mx_row), (6, -b_row)):
            for k, part in enumerate(_split3(piece)):
                lhs_ref[n, r0 + k:r0 + k + 1, :] = part
        for k, part in enumerate(_split3(u_row)):
            rhs_ref[n, k:k + 1, 0:L] = part
        for k, part in enumerate(_split3(m_prev)):
            rhs_ref[n, k:k + 1, L:2 * L] = jnp.broadcast_to(part, (1, L))
        outer = lax.dot_general(lhs_ref[n].astype(BF16), rhs_ref[n].astype(BF16), TN_DIMS,
                                preferred_element_type=F32)
        e = jnp.exp(outer)
        p = jnp.where(masks[d], e[:, 0:L], 0.0)
        e_inter = e[:, L:2 * L]
        clamp = e[:, 2 * L:3 * L]

        q = q_ref[cols, :]
        kh = kt_ref[hrows, cols]
        kz_ref[n, hrows, :] = kh
        qk = jnp.dot(q, kz_ref[n], preferred_element_type=F32)
        sqk = (qk * p).astype(BF16)
        vaug = jnp.concatenate([v_ref[cols, hcols], ones_v], axis=1)
        ke = (kh.astype(F32) * jnp.exp(u_row - umax)).astype(BF16)
        both = jnp.dot(jnp.concatenate([sqk, ke], axis=0), vaug, preferred_element_type=F32)
        intra = both[0:L]
        c_loc = both[L:L + QK_DIM]
        inter = jnp.dot(q, cext_ref[n], preferred_element_type=F32)
        nd = intra + jnp.concatenate([e_inter, e_inter], axis=1) * inter
        hout = nd[:, 0:V_DIM] / jnp.maximum(jnp.abs(nd[:, V_DIM:2 * V_DIM]), clamp)
        hacc_ref[cols, hcols] = hacc_ref[cols, hcols] + hout

        m_loc = g_tot + umax
        m_new = jnp.maximum(g_tot + m_prev, m_loc)
        a = jnp.exp(g_tot + m_prev - m_new)
        bb = jnp.exp(m_loc - m_new)
        c_new = a * c_ref[n] + bb * c_loc
        c_ref[n] = c_new
        cext_ref[n, hrows, :] = c_new.astype(BF16)
        return m_new

    def body(it, carry):
        new = []
        for d in range(2):
            c = it if d == 0 else nc - 1 - it
            for h in range(HEADS):
                new.append(chunk_step(d, h, c, carry[d * HEADS + h]))
        return tuple(new)

    zero = jnp.zeros((1, 1), F32)
    lax.fori_loop(0, nc, body, (zero,) * (2 * HEADS), unroll=4)

    for h in range(HEADS):
        hcols = slice(h * V_DIM, (h + 1) * V_DIM)
        hs = hacc_ref[:, hcols]
        hn = hs * lax.rsqrt(jnp.mean(hs * hs, axis=-1, keepdims=True) + EPS) * g_ref[:, hcols]
        out_ref[:, hcols] = (jax.nn.sigmoid(o_ref_in[:, hcols].astype(F32)) * hn).astype(out_ref.dtype)


def _mlstm(lq, lkt, lv, lo, lgt, lstm_g, batch, seq):
    t = batch * seq
    kernel = functools.partial(_mlstm_kernel, seq=seq)
    L = CHUNK
    n_rec = 2 * HEADS
    gate_rows = pltpu.VMEM((N_GATES, seq), F32)
    return pl.pallas_call(
        kernel,
        grid=(batch,),
        in_specs=[
            pl.BlockSpec((seq, 256), lambda b: (b, 0)),
            pl.BlockSpec((256, seq), lambda b: (0, b)),
            pl.BlockSpec((seq, LSTM_WIDTH), lambda b: (b, 0)),
            pl.BlockSpec((seq, LSTM_WIDTH), lambda b: (b, 0)),
            pl.BlockSpec((N_GATES, seq), lambda b: (0, b)),
            pl.BlockSpec((1, LSTM_WIDTH), lambda b: (0, 0)),
        ],
        out_specs=pl.BlockSpec((seq, LSTM_WIDTH), lambda b: (b, 0)),
        out_shape=jax.ShapeDtypeStruct((t, LSTM_WIDTH), BF16),
        scratch_shapes=[
            gate_rows,
            gate_rows,
            gate_rows,
            pltpu.VMEM((seq, LSTM_WIDTH), F32),
            pltpu.VMEM((n_rec, 256, L), BF16),
            pltpu.VMEM((n_rec, QK_DIM, 2 * V_DIM), F32),
            pltpu.VMEM((n_rec, 256, 2 * V_DIM), BF16),
            pltpu.VMEM((n_rec, 16, L), F32),
            pltpu.VMEM((n_rec, 16, 3 * L), F32),
        ],
        compiler_params=_params(("parallel",)),
        name="mlstm",
    )(lq, lkt, lv, lo, lgt, lstm_g)


def _outproj_kernel(att_ref, lstm_ref, x_ref, w_ref, g_ref, b_ref, o_ref):
    y = jnp.dot(att_ref[...], w_ref[0:ATT_WIDTH, :], preferred_element_type=F32)
    y = y + jnp.dot(lstm_ref[...], w_ref[ATT_WIDTH:ATT_WIDTH + LSTM_WIDTH, :], preferred_element_type=F32)
    o_ref[...] = _layer_norm(ALPHA * x_ref[...] + y, g_ref[...], b_ref[...])


def _outproj(att, lstm, x2d, w_out, g, b, tm):
    t = x2d.shape[0]
    row_blk = lambda n: pl.BlockSpec((tm, n), lambda i: (i, 0))
    full = lambda a: pl.BlockSpec(a.shape, lambda i: (0, 0))
    return pl.pallas_call(
        _outproj_kernel,
        grid=(t // tm,),
        in_specs=[row_blk(ATT_WIDTH), row_blk(LSTM_WIDTH), row_blk(D_MODEL), full(w_out), full(g), full(b)],
        out_specs=row_blk(D_MODEL),
        out_shape=jax.ShapeDtypeStruct((t, D_MODEL), F32),
        compiler_params=_params(("parallel",)),
        name="outproj_ln1",
    )(att, lstm, x2d, w_out, g, b)


HALO = 8


def _ffn_kernel(x_ref, xp_ref, xn_ref, wg_ref, wu_ref, cw_ref, cb_ref, wd_ref, g_ref, b_ref, o_ref,
                gs_ref, *, seq, tm):
    i = pl.program_id(0)
    xb = x_ref[...].astype(BF16)
    has_prev = ((i * tm) % seq != 0).astype(F32)
    has_next = (((i + 1) * tm) % seq != 0).astype(F32)
    halo = jnp.concatenate([xp_ref[...] * has_prev, xn_ref[...] * has_next], axis=0).astype(BF16)
    acc = None
    for j in range(len(FF_BOUNDS) - 1):
        cs = slice(FF_BOUNDS[j], FF_BOUNDS[j + 1])
        gs_ref[HALO:HALO + tm, cs] = jnp.dot(xb, wg_ref[:, cs], preferred_element_type=F32)
        gh = jnp.dot(halo, wg_ref[:, cs], preferred_element_type=F32)
        gs_ref[0:HALO, cs] = gh[0:HALO]
        gs_ref[HALO + tm:2 * HALO + tm, cs] = gh[HALO:2 * HALO]
        conv = (cb_ref[:, cs]
                + gs_ref[HALO - 1:HALO - 1 + tm, cs] * cw_ref[0:1, cs]
                + gs_ref[HALO:HALO + tm, cs] * cw_ref[1:2, cs]
                + gs_ref[HALO + 1:HALO + 1 + tm, cs] * cw_ref[2:3, cs])
        up = jnp.dot(xb, wu_ref[:, cs], preferred_element_type=F32)
        gelu = 0.5 * conv * (lax.erf(conv * (1.0 / math.sqrt(2.0))) + 1.0)
        hmid = (gelu * up).astype(BF16)
        part = jnp.dot(hmid, wd_ref[cs, :], preferred_element_type=F32)
        acc = part if acc is None else acc + part
    o_ref[...] = _layer_norm(ALPHA * x_ref[...] + acc, g_ref[...], b_ref[...])


def _ffn(x2d, w_g, w_u, conv_w, conv_b, w_d, g, b, seq, tm):
    t = x2d.shape[0]
    hb = tm // HALO
    last_hblk = t // HALO - 1
    kernel = functools.partial(_ffn_kernel, seq=seq, tm=tm)
    resident = lambda a: pl.BlockSpec(a.shape, lambda i: (0, 0), pipeline_mode=pl.Buffered(1))
    return pl.pallas_call(
        kernel,
        grid=(t // tm,),
        in_specs=[
            pl.BlockSpec((tm, D_MODEL), lambda i: (i, 0)),
            pl.BlockSpec((HALO, D_MODEL), lambda i: (jnp.maximum(i * hb - 1, 0), 0)),
            pl.BlockSpec((HALO, D_MODEL), lambda i: (jnp.minimum((i + 1) * hb, last_hblk), 0)),
            resident(w_g), resident(w_u), resident(conv_w), resident(conv_b), resident(w_d),
            resident(g), resident(b),
        ],
        out_specs=pl.BlockSpec((tm, D_MODEL), lambda i: (i, 0)),
        out_shape=jax.ShapeDtypeStruct((t, D_MODEL), F32),
        scratch_shapes=[
            pltpu.VMEM((tm + 2 * HALO, D_FF), F32),
        ],
        compiler_params=_params(("parallel",)),
        name="conv_ffn_ln2",
    )(x2d, x2d, x2d, w_g, w_u, conv_w, conv_b, w_d, g, b)


def _prep_layer(l, w_in, gate_bias, lam_q1, lam_k1, lam_q2, lam_k2, att_norm_g, lstm_norm_g, w_out,
                ln1_g, ln1_b, w_gu, conv_w, conv_b, w_down, ln2_g, ln2_b):
    wl = w_in[l]
    row = lambda a: a.reshape(1, -1).astype(F32)
    return dict(
        w_main=wl[:, :C_LG].astype(BF16),
        w_akt=wl[:, C_AK:C_AV].T.astype(BF16),
        w_kt=wl[:, C_LK:C_LV].T.astype(BF16),
        w_gt=wl[:, C_LG:C_END].T.astype(BF16),
        gbias=gate_bias[l].reshape(N_GATES, 1).astype(F32),
        lq1=row(lam_q1[l]), lk1=row(lam_k1[l]), lq2=row(lam_q2[l]), lk2=row(lam_k2[l]),
        att_g=row(att_norm_g[l]), lstm_g=row(lstm_norm_g[l]),
        w_out=w_out[l].astype(BF16), ln1_g=row(ln1_g[l]), ln1_b=row(ln1_b[l]),
        w_g=w_gu[l][:, :D_FF].astype(BF16), w_u=w_gu[l][:, D_FF:].astype(BF16),
        conv_w=conv_w[l].astype(F32), conv_b=row(conv_b[l]),
        w_d=w_down[l].astype(BF16), ln2_g=row(ln2_g[l]), ln2_b=row(ln2_b[l]),
    )


def _trunk(x, layers, tm_proj=PROJ_BLOCK, tm=TOKEN_BLOCK):
    batch, seq, _ = x.shape
    x2d = x.reshape(batch * seq, D_MODEL)
    for l, p in enumerate(layers):
        lam_init = 0.8 - 0.6 * math.exp(-0.3 * l)
        aq, akt, av, lq, lkt, lv, lo, lgt = _inproj(x2d, p["w_main"], p["w_akt"], p["w_kt"], p["w_gt"],
                                                    p["gbias"], tm_proj)
        att = _attention(aq, akt, av, p["lq1"], p["lk1"], p["lq2"], p["lk2"], p["att_g"], batch, seq, lam_init)
        lstm = _mlstm(lq, lkt, lv, lo, lgt, p["lstm_g"], batch, seq)
        x2d = _outproj(att, lstm, x2d, p["w_out"], p["ln1_g"], p["ln1_b"], tm_proj)
        x2d = _ffn(x2d, p["w_g"], p["w_u"], p["conv_w"], p["conv_b"], p["w_d"], p["ln2_g"], p["ln2_b"], seq, tm)
    return x2d.reshape(batch, seq, D_MODEL)


def kernel(x_prompt, x_sample, w_in, gate_bias, lam_q1, lam_k1, lam_q2, lam_k2, att_norm_g, lstm_norm_g,
           w_out, ln1_g, ln1_b, w_gu, conv_w, conv_b, w_down, ln2_g, ln2_b):
    layers = [_prep_layer(l, w_in, gate_bias, lam_q1, lam_k1, lam_q2, lam_k2, att_norm_g, lstm_norm_g,
                          w_out, ln1_g, ln1_b, w_gu, conv_w, conv_b, w_down, ln2_g, ln2_b)
              for l in range(DEPTH)]
    return (_trunk(x_prompt, layers), _trunk(x_sample, layers))
```

```python
import functools
import math

import jax
import jax.numpy as jnp
from jax import lax
from jax.experimental import pallas as pl
from jax.experimental.pallas import tpu as pltpu

F32 = jnp.float32
BF16 = jnp.bfloat16

D_MODEL = 1024
DEPTH = 2
HEADS = 4
QK_DIM = 64
V_DIM = 128
ATT_WIDTH = HEADS * V_DIM
LSTM_WIDTH = HEADS * V_DIM
N_GATES = 4 * HEADS
D_FF = 2816
CHUNK = 128
ALPHA = (2 * DEPTH) ** 0.25
EPS = 1e-5
QK_SCALE = QK_DIM ** -0.5

C_AQ, C_AK, C_AV, C_LQ, C_LK, C_LV, C_LO, C_LG, C_END = 0, 512, 1024, 1536, 1792, 2048, 2560, 3072, 3088

LANES = 128
MXU_TILE = 256
VMEM_LIMIT = 56 * 1024 * 1024

PROJ_BLOCK = 1024
TOKEN_BLOCK = 512
FF_BOUNDS = (0, 6 * MXU_TILE, D_FF)
K_TILE = MXU_TILE
SCORE_BUF_BYTES = 16 * 1024 * 1024


def _attn_sub_blocks(seq):
    return min(SCORE_BUF_BYTES // (2 * K_TILE * seq * 4), seq // K_TILE)

NT_DIMS = (((1,), (1,)), ((), ()))
TN_DIMS = (((0,), (0,)), ((), ()))


def _params(sem):
    return pltpu.CompilerParams(dimension_semantics=sem, vmem_limit_bytes=VMEM_LIMIT)


def _layer_norm(z, g, b):
    mu = jnp.mean(z, axis=-1, keepdims=True)
    zc = z - mu
    var = jnp.mean(zc * zc, axis=-1, keepdims=True)
    return zc * lax.rsqrt(var + EPS) * g + b


def _inproj_kernel(x_ref, w_ref, wakt_ref, wkt_ref, wgt_ref, gb_ref,
                   aq_ref, akt_ref, av_ref, lq_ref, lkt_ref, lv_ref, lo_ref, lgt_ref):
    xb = x_ref[...].astype(BF16)

    def seg(a, b):
        return jnp.dot(xb, w_ref[:, a:b], preferred_element_type=F32)

    aq_ref[...] = (seg(C_AQ, C_AK) * QK_SCALE).astype(BF16)
    akt_ref[...] = lax.dot_general(wakt_ref[...], xb, NT_DIMS, preferred_element_type=F32).astype(BF16)
    av_ref[...] = seg(C_AV, C_LQ).astype(BF16)
    lq_ref[...] = seg(C_LQ, C_LK).astype(BF16)
    kt = lax.dot_general(wkt_ref[...], xb, NT_DIMS, preferred_element_type=F32)
    lkt_ref[...] = (kt * QK_SCALE).astype(BF16)
    lv_ref[...] = seg(C_LV, C_LO).astype(BF16)
    lo_ref[...] = seg(C_LO, C_LG).astype(BF16)
    g = lax.dot_general(wgt_ref[...], xb, NT_DIMS, preferred_element_type=F32) + gb_ref[...]
    row = lax.broadcasted_iota(jnp.int32, g.shape, 0)
    is_forget = ((row >> 2) & 1) == 1
    log_sig = jnp.minimum(g, 0.0) - jnp.log1p(jnp.exp(-jnp.abs(g)))
    lgt_ref[...] = jnp.where(is_forget, log_sig, g)


def _inproj(x2d, w_main, w_akt, w_kt, w_gt, gbias, tm):
    t = x2d.shape[0]
    grid = (t // tm,)
    row_blk = lambda n: pl.BlockSpec((tm, n), lambda i: (i, 0))
    full = lambda a: pl.BlockSpec(a.shape, lambda i: (0, 0))
    out_shapes = (
        jax.ShapeDtypeStruct((t, 512), BF16),
        jax.ShapeDtypeStruct((512, t), BF16),
        jax.ShapeDtypeStruct((t, ATT_WIDTH), BF16),
        jax.ShapeDtypeStruct((t, 256), BF16),
        jax.ShapeDtypeStruct((256, t), BF16),
        jax.ShapeDtypeStruct((t, LSTM_WIDTH), BF16),
        jax.ShapeDtypeStruct((t, LSTM_WIDTH), BF16),
        jax.ShapeDtypeStruct((N_GATES, t), F32),
    )
    out_specs = (
        row_blk(512), pl.BlockSpec((512, tm), lambda i: (0, i)), row_blk(ATT_WIDTH), row_blk(256),
        pl.BlockSpec((256, tm), lambda i: (0, i)),
        row_blk(LSTM_WIDTH), row_blk(LSTM_WIDTH),
        pl.BlockSpec((N_GATES, tm), lambda i: (0, i)),
    )
    return pl.pallas_call(
        _inproj_kernel,
        grid=grid,
        in_specs=[row_blk(D_MODEL), full(w_main), full(w_akt), full(w_kt), full(w_gt), full(gbias)],
        out_specs=out_specs,
        out_shape=out_shapes,
        compiler_params=_params(("parallel",)),
        name="inproj",
    )(x2d, w_main, w_akt, w_kt, w_gt, gbias)


def _alibi_slope(h):
    return jnp.where(h == 0, 0.25, jnp.where(h == 1, 0.0625, jnp.where(h == 2, 0.015625, 0.00390625))).astype(F32)


def _attn_kernel(q_ref, k_ref, v_ref, lq1_ref, lk1_ref, lq2_ref, lk2_ref, g_ref, o_ref,
                 kcat_ref, vaug_ref, qv_ref, sa_ref, sb_ref, *, seq, n_blocks, lam_init):
    SUB = _attn_sub_blocks(seq)
    Q_BLOCK = SUB * K_TILE
    n_tiles = seq // K_TILE
    nq = seq // Q_BLOCK
    step = pl.program_id(0)
    blk = jnp.minimum(step, n_blocks - 1)
    head_idx = blk // nq
    h = head_idx % HEADS
    qi = blk % nq
    prev_head_idx = jnp.maximum(step - 1, 0) // nq
    slope = _alibi_slope(h)

    @pl.when(step == 0)
    def _init_constants():
        sb_ref[...] = jnp.zeros_like(sb_ref)
        frow = lax.broadcasted_iota(jnp.int32, (LANES, seq), 0)
        j = lax.broadcasted_iota(jnp.int32, (LANES, seq), 1)
        kpos = jnp.where(frow < 2, 1.0,
                         jnp.where(frow == 2, (j >> 6).astype(F32),
                                   jnp.where(frow == 3, (j & 63).astype(F32), 0.0))).astype(BF16)
        zeros = jnp.zeros((QK_DIM, K_TILE), BF16)
        for t in range(n_tiles):
            cols = slice(t * K_TILE, (t + 1) * K_TILE)
            kcat_ref[t, QK_DIM:LANES, 0:K_TILE] = zeros
            kcat_ref[t, 0:QK_DIM, K_TILE:2 * K_TILE] = zeros
            kcat_ref[t, LANES:2 * LANES, 0:K_TILE] = kpos[:, cols]
            kcat_ref[t, LANES:2 * LANES, K_TILE:2 * K_TILE] = kpos[:, cols]
        for slot in range(2):
            vaug_ref[slot, :, V_DIM:2 * V_DIM] = jnp.ones((seq, V_DIM), BF16)

    @pl.when((qi == 0) & (step < n_blocks))
    def _load_keys():
        for t in range(n_tiles):
            cols = slice(t * K_TILE, (t + 1) * K_TILE)
            kcat_ref[t, 0:QK_DIM, 0:K_TILE] = k_ref[0:QK_DIM, cols]
            kcat_ref[t, QK_DIM:LANES, K_TILE:2 * K_TILE] = k_ref[QK_DIM:LANES, cols]
        vaug_ref[head_idx % 2, :, 0:V_DIM] = v_ref[...]

    def block_step(s_new, s_old):
        q = q_ref[...]
        lane = lax.broadcasted_iota(jnp.int32, (Q_BLOCK, LANES), 1)
        row = lax.broadcasted_iota(jnp.int32, (Q_BLOCK, LANES), 0)
        i = qi * Q_BLOCK + row
        feat = slope * jnp.where(lane == 0, -64.0 * (i >> 6).astype(F32),
                                 jnp.where(lane == 1, -(i & 63).astype(F32),
                                           jnp.where(lane == 2, 64.0, jnp.where(lane == 3, 1.0, 0.0))))
        sub = row // K_TILE
        for var in range(SUB + 2):
            if var == 0:
                coef = 1.0
            elif var == SUB + 1:
                coef = -1.0
            else:
                coef = jnp.where(sub > var - 1, 1.0, jnp.where(sub == var - 1, 0.0, -1.0))
            qv_ref[var, :, 0:LANES] = q
            qv_ref[var, :, LANES:2 * LANES] = (coef * feat).astype(BF16)

        vslot = prev_head_idx % 2
        row_max = []
        for m in range(2):
            mx = s_old[m, 0]
            for t in range(1, n_tiles):
                mx = jnp.maximum(mx, s_old[m, t])
            row_max.append(jnp.broadcast_to(jnp.max(mx, axis=1, keepdims=True), (Q_BLOCK, K_TILE)))

        acc = jnp.zeros((2 * Q_BLOCK, 2 * V_DIM), F32)
        for t in range(n_tiles):
            p = jnp.concatenate([jnp.exp(s_old[m, t] - row_max[m]).astype(BF16) for m in range(2)], axis=0)
            acc = acc + jnp.dot(p, vaug_ref[vslot, t * K_TILE:(t + 1) * K_TILE, :],
                                preferred_element_type=F32)
            sel = jnp.clip(t - SUB * qi + 1, 0, SUB + 1)
            s = jnp.dot(qv_ref[sel], kcat_ref[t], preferred_element_type=F32)
            s_new[0, t] = s[:, 0:K_TILE]
            s_new[1, t] = s[:, K_TILE:2 * K_TILE]
        r = lax.broadcasted_iota(jnp.int32, (K_TILE, K_TILE), 0)
        c = lax.broadcasted_iota(jnp.int32, (K_TILE, K_TILE), 1)
        diag_bias = -slope * jnp.abs(r - c).astype(F32)
        for sb in range(SUB):
            rows = slice(sb * K_TILE, (sb + 1) * K_TILE)
            for m in range(2):
                s_new[m, SUB * qi + sb, rows, :] = s_new[m, SUB * qi + sb, rows, :] + diag_bias

        outs = [acc[m * Q_BLOCK:(m + 1) * Q_BLOCK, 0:V_DIM] / acc[m * Q_BLOCK:(m + 1) * Q_BLOCK, V_DIM:2 * V_DIM]
                for m in range(2)]
        lam = (jnp.exp(jnp.sum(lq1_ref[...] * lk1_ref[...], keepdims=True))
               - jnp.exp(jnp.sum(lq2_ref[...] * lk2_ref[...], keepdims=True)) + lam_init)
        att = outs[0] - lam * outs[1]
        att = att * lax.rsqrt(jnp.mean(att * att, axis=-1, keepdims=True) + EPS)
        o_ref[...] = (att * g_ref[...] * (1.0 - lam_init)).astype(o_ref.dtype)

    @pl.when(step % 2 == 0)
    def _even():
        block_step(sa_ref, sb_ref)

    @pl.when(step % 2 == 1)
    def _odd():
        block_step(sb_ref, sa_ref)


def _attention(aq, akt, av, lq1, lk1, lq2, lk2, att_g, batch, seq, lam_init):
    t = batch * seq
    SUB = _attn_sub_blocks(seq)
    Q_BLOCK = SUB * K_TILE
    nq = seq // Q_BLOCK
    n_tiles = seq // K_TILE
    n_blocks = batch * HEADS * nq

    def cur(i):
        blk = jnp.minimum(i, n_blocks - 1)
        return blk // (HEADS * nq), (blk // nq) % HEADS, blk % nq

    def prev(i):
        blk = jnp.maximum(i - 1, 0)
        return blk // (HEADS * nq), (blk // nq) % HEADS, blk % nq

    def q_map(i):
        b, h, q = cur(i)
        return b * nq + q, h

    def kt_map(i):
        b, h, _ = cur(i)
        return h, b

    def v_map(i):
        b, h, _ = cur(i)
        return b, h

    def out_map(i):
        b, h, q = prev(i)
        return b * nq + q, h

    small = pl.BlockSpec((1, QK_DIM), lambda i: (0, 0))
    kernel = functools.partial(_attn_kernel, seq=seq, n_blocks=n_blocks, lam_init=lam_init)
    score_buf = pltpu.VMEM((2, n_tiles, Q_BLOCK, K_TILE), F32)
    return pl.pallas_call(
        kernel,
        grid=(n_blocks + 1,),
        in_specs=[
            pl.BlockSpec((Q_BLOCK, LANES), q_map),
            pl.BlockSpec((LANES, seq), kt_map),
            pl.BlockSpec((seq, V_DIM), v_map),
            small, small, small, small,
            pl.BlockSpec((1, V_DIM), lambda i: (0, prev(i)[1])),
        ],
        out_specs=pl.BlockSpec((Q_BLOCK, V_DIM), out_map),
        out_shape=jax.ShapeDtypeStruct((t, ATT_WIDTH), BF16),
        scratch_shapes=[
            pltpu.VMEM((n_tiles, 2 * LANES, 2 * K_TILE), BF16),
            pltpu.VMEM((2, seq, 2 * V_DIM), BF16),
            pltpu.VMEM((SUB + 2, Q_BLOCK, 2 * LANES), BF16),
            score_buf, score_buf,
        ],
        compiler_params=_params(("arbitrary",)),
        name="diff_attention",
    )(aq, akt, av, lq1, lk1, lq2, lk2, att_g)


def _split3(x):
    hi = x.astype(BF16).astype(F32)
    r1 = x - hi
    mid = r1.astype(BF16).astype(F32)
    lo = (r1 - mid).astype(BF16).astype(F32)
    return hi, mid, lo


def _chunk_scan(x, pos, combine, fill, reverse, seq):
    k = 1
    while k < CHUNK:
        if reverse:
            shifted = pltpu.roll(x, seq - k, 1)
            ok = pos < CHUNK - k
        else:
            shifted = pltpu.roll(x, k, 1)
            ok = pos >= k
        x = combine(x, jnp.where(ok, shifted, fill))
        k *= 2
    return x


def _mlstm_kernel(q_ref, kt_ref, v_ref, o_ref_in, gt_ref, g_ref, out_ref,
                  u_ref, bs_ref, cm_ref, hacc_ref, kz_ref, c_ref, cext_ref, lhs_ref, rhs_ref, *, seq):
    L = CHUNK
    nc = seq // L

    gts = gt_ref[...]
    row = lax.broadcasted_iota(jnp.int32, gts.shape, 0)
    pos = lax.broadcasted_iota(jnp.int32, gts.shape, 1) & (L - 1)
    psum = _chunk_scan(gts, pos, jnp.add, 0.0, False, seq)
    ssum = _chunk_scan(gts, pos, jnp.add, 0.0, True, seq)
    bsum = jnp.where(row < 2 * HEADS, psum, ssum)
    u = gts - pltpu.roll(bsum, N_GATES - HEADS, 0)
    cmax = jnp.where(row < 2 * HEADS,
                     _chunk_scan(u, pos, jnp.maximum, -jnp.inf, False, seq),
                     _chunk_scan(u, pos, jnp.maximum, -jnp.inf, True, seq))
    u_ref[...] = u
    bs_ref[...] = bsum
    cm_ref[...] = cmax

    hacc_ref[...] = jnp.zeros_like(hacc_ref)
    kz_ref[...] = jnp.zeros_like(kz_ref)
    c_ref[...] = jnp.zeros_like(c_ref)
    cext_ref[...] = jnp.zeros_like(cext_ref)
    lrow = lax.broadcasted_iota(jnp.int32, (16, L), 0)
    lhs_const = jnp.where(lrow < 3, 1.0, 0.0).astype(F32)
    rrow = lax.broadcasted_iota(jnp.int32, (16, 3 * L), 0)
    rcol = lax.broadcasted_iota(jnp.int32, (16, 3 * L), 1) >> 7
    rhs_const = jnp.where(((rrow >= 3) & (rrow < 6)) | ((rrow >= 6) & (rrow < 9) & (rcol == 2)),
                          1.0, 0.0).astype(F32)
    for n in range(2 * HEADS):
        lhs_ref[n] = lhs_const
        rhs_ref[n] = rhs_const

    jj = lax.broadcasted_iota(jnp.int32, (L, L), 0)
    ss = lax.broadcasted_iota(jnp.int32, (L, L), 1)
    masks = (ss <= jj, ss >= jj)
    ones_v = jnp.ones((L, V_DIM), BF16)

    def chunk_matmuls(d, h, c, m_prev):
        n = d * HEADS + h
        c0 = pl.multiple_of(c * L, L)
        cols = pl.ds(c0, L)
        hrows = slice(h * QK_DIM, (h + 1) * QK_DIM)
        u_row = u_ref[8 * d + h:8 * d + h + 1, cols]
        b_row = bs_ref[8 * d + HEADS + h:8 * d + HEADS + h + 1, cols]
        cm_row = cm_ref[8 * d + h:8 * d + h + 1, cols]
        umax = jnp.max(u_row, axis=1, keepdims=True)
        g_tot = b_row[:, L - 1:L] if d == 0 else b_row[:, 0:1]
        mx_row = jnp.maximum(m_prev, cm_row)

        for r0, piece in ((3, -mx_row), (6, -b_row)):
            for k, part in enumerate(_split3(piece)):
                lhs_ref[n, r0 + k:r0 + k + 1, :] = part
        for k, part in enumerate(_split3(u_row)):
            rhs_ref[n, k:k + 1, 0:L] = part
        for k, part in enumerate(_split3(m_prev)):
            rhs_ref[n, k:k + 1, L:2 * L] = jnp.broadcast_to(part, (1, L))
        outer = lax.dot_general(lhs_ref[n].astype(BF16), rhs_ref[n].astype(BF16), TN_DIMS,
                                preferred_element_type=F32)
        q = q_ref[cols, :]
        kh = kt_ref[hrows, cols]
        kz_ref[n, hrows, :] = kh
        qk = jnp.dot(q, kz_ref[n], preferred_element_type=F32)
        inter = jnp.dot(q, cext_ref[n], preferred_element_type=F32)
        return (d, h, cols, u_row, umax, g_tot, m_prev, outer, kh, qk, inter)

    def chunk_finish(stage1):
        d, h, cols, u_row, umax, g_tot, m_prev, outer, kh, qk, inter = stage1
        n = d * HEADS + h
        hrows = slice(h * QK_DIM, (h + 1) * QK_DIM)
        hcols = slice(h * V_DIM, (h + 1) * V_DIM)
        e = jnp.exp(outer)
        p = jnp.where(masks[d], e[:, 0:L], 0.0)
        e_inter = e[:, L:2 * L]
        clamp = e[:, 2 * L:3 * L]
        sqk = (qk * p).astype(BF16)
        vaug = jnp.concatenate([v_ref[cols, hcols], ones_v], axis=1)
        ke = (kh.astype(F32) * jnp.exp(u_row - umax)).astype(BF16)
        both = jnp.dot(jnp.concatenate([sqk, ke], axis=0), vaug, preferred_element_type=F32)
        intra = both[0:L]
        c_loc = both[L:L + QK_DIM]
        nd = intra + jnp.concatenate([e_inter, e_inter], axis=1) * inter
        hout = nd[:, 0:V_DIM] / jnp.maximum(jnp.abs(nd[:, V_DIM:2 * V_DIM]), clamp)
        hacc_ref[cols, hcols] = hacc_ref[cols, hcols] + hout

        m_loc = g_tot + umax
        m_new = jnp.maximum(g_tot + m_prev, m_loc)
        a = jnp.exp(g_tot + m_prev - m_new)
        bb = jnp.exp(m_loc - m_new)
        c_new = a * c_ref[n] + bb * c_loc
        c_ref[n] = c_new
        cext_ref[n, hrows, :] = c_new.astype(BF16)
        return m_new

    def body(it, carry):
        stage1 = []
        for d in range(2):
            c = it if d == 0 else nc - 1 - it
            for h in range(HEADS):
                stage1.append(chunk_matmuls(d, h, c, carry[d * HEADS + h]))
        return tuple(chunk_finish(s1) for s1 in stage1)

    zero = jnp.zeros((1, 1), F32)
    lax.fori_loop(0, nc, body, (zero,) * (2 * HEADS), unroll=4)

    for h in range(HEADS):
        hcols = slice(h * V_DIM, (h + 1) * V_DIM)
        hs = hacc_ref[:, hcols]
        hn = hs * lax.rsqrt(jnp.mean(hs * hs, axis=-1, keepdims=True) + EPS) * g_ref[:, hcols]
        out_ref[:, hcols] = (jax.nn.sigmoid(o_ref_in[:, hcols].astype(F32)) * hn).astype(out_ref.dtype)


def _mlstm(lq, lkt, lv, lo, lgt, lstm_g, batch, seq):
    t = batch * seq
    kernel = functools.partial(_mlstm_kernel, seq=seq)
    L = CHUNK
    n_rec = 2 * HEADS
    gate_rows = pltpu.VMEM((N_GATES, seq), F32)
    return pl.pallas_call(
        kernel,
        grid=(batch,),
        in_specs=[
            pl.BlockSpec((seq, 256), lambda b: (b, 0)),
            pl.BlockSpec((256, seq), lambda b: (0, b)),
            pl.BlockSpec((seq, LSTM_WIDTH), lambda b: (b, 0)),
            pl.BlockSpec((seq, LSTM_WIDTH), lambda b: (b, 0)),
            pl.BlockSpec((N_GATES, seq), lambda b: (0, b)),
            pl.BlockSpec((1, LSTM_WIDTH), lambda b: (0, 0)),
        ],
        out_specs=pl.BlockSpec((seq, LSTM_WIDTH), lambda b: (b, 0)),
        out_shape=jax.ShapeDtypeStruct((t, LSTM_WIDTH), BF16),
        scratch_shapes=[
            gate_rows,
            gate_rows,
            gate_rows,
            pltpu.VMEM((seq, LSTM_WIDTH), F32),
            pltpu.VMEM((n_rec, 256, L), BF16),
            pltpu.VMEM((n_rec, QK_DIM, 2 * V_DIM), F32),
            pltpu.VMEM((n_rec, 256, 2 * V_DIM), BF16),
            pltpu.VMEM((n_rec, 16, L), F32),
            pltpu.VMEM((n_rec, 16, 3 * L), F32),
        ],
        compiler_params=_params(("parallel",)),
        name="mlstm",
    )(lq, lkt, lv, lo, lgt, lstm_g)


def _outproj_kernel(att_ref, lstm_ref, x_ref, w_ref, g_ref, b_ref, o_ref):
    y = jnp.dot(att_ref[...], w_ref[0:ATT_WIDTH, :], preferred_element_type=F32)
    y = y + jnp.dot(lstm_ref[...], w_ref[ATT_WIDTH:ATT_WIDTH + LSTM_WIDTH, :], preferred_element_type=F32)
    o_ref[...] = _layer_norm(ALPHA * x_ref[...] + y, g_ref[...], b_ref[...])


def _outproj(att, lstm, x2d, w_out, g, b, tm):
    t = x2d.shape[0]
    row_blk = lambda n: pl.BlockSpec((tm, n), lambda i: (i, 0))
    full = lambda a: pl.BlockSpec(a.shape, lambda i: (0, 0))
    return pl.pallas_call(
        _outproj_kernel,
        grid=(t // tm,),
        in_specs=[row_blk(ATT_WIDTH), row_blk(LSTM_WIDTH), row_blk(D_MODEL), full(w_out), full(g), full(b)],
        out_specs=row_blk(D_MODEL),
        out_shape=jax.ShapeDtypeStruct((t, D_MODEL), F32),
        compiler_params=_params(("parallel",)),
        name="outproj_ln1",
    )(att, lstm, x2d, w_out, g, b)


HALO = 8


def _ffn_kernel(x_ref, xp_ref, xn_ref, wg_ref, wu_ref, cw_ref, cb_ref, wd_ref, g_ref, b_ref, o_ref,
                gs_ref, *, seq, tm):
    i = pl.program_id(0)
    xb = x_ref[...].astype(BF16)
    has_prev = ((i * tm) % seq != 0).astype(F32)
    has_next = (((i + 1) * tm) % seq != 0).astype(F32)
    halo = jnp.concatenate([xp_ref[...] * has_prev, xn_ref[...] * has_next], axis=0).astype(BF16)
    acc = None
    for j in range(len(FF_BOUNDS) - 1):
        cs = slice(FF_BOUNDS[j], FF_BOUNDS[j + 1])
        gs_ref[HALO:HALO + tm, cs] = jnp.dot(xb, wg_ref[:, cs], preferred_element_type=F32)
        gh = jnp.dot(halo, wg_ref[:, cs], preferred_element_type=F32)
        gs_ref[0:HALO, cs] = gh[0:HALO]
        gs_ref[HALO + tm:2 * HALO + tm, cs] = gh[HALO:2 * HALO]
        conv = (cb_ref[:, cs]
                + gs_ref[HALO - 1:HALO - 1 + tm, cs] * cw_ref[0:1, cs]
                + gs_ref[HALO:HALO + tm, cs] * cw_ref[1:2, cs]
                + gs_ref[HALO + 1:HALO + 1 + tm, cs] * cw_ref[2:3, cs])
        up = jnp.dot(xb, wu_ref[:, cs], preferred_element_type=F32)
        gelu = 0.5 * conv * (lax.erf(conv * (1.0 / math.sqrt(2.0))) + 1.0)
        hmid = (gelu * up).astype(BF16)
        part = jnp.dot(hmid, wd_ref[cs, :], preferred_element_type=F32)
        acc = part if acc is None else acc + part
    o_ref[...] = _layer_norm(ALPHA * x_ref[...] + acc, g_ref[...], b_ref[...])


def _ffn(x2d, w_g, w_u, conv_w, conv_b, w_d, g, b, seq, tm):
    t = x2d.shape[0]
    hb = tm // HALO
    last_hblk = t // HALO - 1
    kernel = functools.partial(_ffn_kernel, seq=seq, tm=tm)
    resident = lambda a: pl.BlockSpec(a.shape, lambda i: (0, 0), pipeline_mode=pl.Buffered(1))
    return pl.pallas_call(
        kernel,
        grid=(t // tm,),
        in_specs=[
            pl.BlockSpec((tm, D_MODEL), lambda i: (i, 0)),
            pl.BlockSpec((HALO, D_MODEL), lambda i: (jnp.maximum(i * hb - 1, 0), 0)),
            pl.BlockSpec((HALO, D_MODEL), lambda i: (jnp.minimum((i + 1) * hb, last_hblk), 0)),
            resident(w_g), resident(w_u), resident(conv_w), resident(conv_b), resident(w_d),
            resident(g), resident(b),
        ],
        out_specs=pl.BlockSpec((tm, D_MODEL), lambda i: (i, 0)),
        out_shape=jax.ShapeDtypeStruct((t, D_MODEL), F32),
        scratch_shapes=[
            pltpu.VMEM((tm + 2 * HALO, D_FF), F32),
        ],
        compiler_params=_params(("parallel",)),
        name="conv_ffn_ln2",
    )(x2d, x2d, x2d, w_g, w_u, conv_w, conv_b, w_d, g, b)


def _prep_layer(l, w_in, gate_bias, lam_q1, lam_k1, lam_q2, lam_k2, att_norm_g, lstm_norm_g, w_out,
                ln1_g, ln1_b, w_gu, conv_w, conv_b, w_down, ln2_g, ln2_b):
    wl = w_in[l]
    row = lambda a: a.reshape(1, -1).astype(F32)
    return dict(
        w_main=wl[:, :C_LG].astype(BF16),
        w_akt=wl[:, C_AK:C_AV].T.astype(BF16),
        w_kt=wl[:, C_LK:C_LV].T.astype(BF16),
        w_gt=wl[:, C_LG:C_END].T.astype(BF16),
        gbias=gate_bias[l].reshape(N_GATES, 1).astype(F32),
        lq1=row(lam_q1[l]), lk1=row(lam_k1[l]), lq2=row(lam_q2[l]), lk2=row(lam_k2[l]),
        att_g=row(att_norm_g[l]), lstm_g=row(lstm_norm_g[l]),
        w_out=w_out[l].astype(BF16), ln1_g=row(ln1_g[l]), ln1_b=row(ln1_b[l]),
        w_g=w_gu[l][:, :D_FF].astype(BF16), w_u=w_gu[l][:, D_FF:].astype(BF16),
        conv_w=conv_w[l].astype(F32), conv_b=row(conv_b[l]),
        w_d=w_down[l].astype(BF16), ln2_g=row(ln2_g[l]), ln2_b=row(ln2_b[l]),
    )


def _trunk(x, layers, tm_proj=PROJ_BLOCK, tm=TOKEN_BLOCK):
    batch, seq, _ = x.shape
    x2d = x.reshape(batch * seq, D_MODEL)
    for l, p in enumerate(layers):
        lam_init = 0.8 - 0.6 * math.exp(-0.3 * l)
        aq, akt, av, lq, lkt, lv, lo, lgt = _inproj(x2d, p["w_main"], p["w_akt"], p["w_kt"], p["w_gt"],
                                                    p["gbias"], tm_proj)
        att = _attention(aq, akt, av, p["lq1"], p["lk1"], p["lq2"], p["lk2"], p["att_g"], batch, seq, lam_init)
        lstm = _mlstm(lq, lkt, lv, lo, lgt, p["lstm_g"], batch, seq)
        x2d = _outproj(att, lstm, x2d, p["w_out"], p["ln1_g"], p["ln1_b"], tm_proj)
        x2d = _ffn(x2d, p["w_g"], p["w_u"], p["conv_w"], p["conv_b"], p["w_d"], p["ln2_g"], p["ln2_b"], seq, tm)
    return x2d.reshape(batch, seq, D_MODEL)


def kernel(x_prompt, x_sample, w_in, gate_bias, lam_q1, lam_k1, lam_q2, lam_k2, att_norm_g, lstm_norm_g,
           w_out, ln1_g, ln1_b, w_gu, conv_w, conv_b, w_down, ln2_g, ln2_b):
    layers = [_prep_layer(l, w_in, gate_bias, lam_q1, lam_k1, lam_q2, lam_k2, att_norm_g, lstm_norm_g,
                          w_out, ln1_g, ln1_b, w_gu, conv_w, conv_b, w_down, ln2_g, ln2_b)
              for l in range(DEPTH)]
    return (_trunk(x_prompt, layers), _trunk(x_sample, layers))
```

```python
import functools
import math

import jax
import jax.numpy as jnp
from jax import lax
from jax.experimental import pallas as pl
from jax.experimental.pallas import tpu as pltpu

F32 = jnp.float32
BF16 = jnp.bfloat16

D_MODEL = 1024
DEPTH = 2
HEADS = 4
QK_DIM = 64
V_DIM = 128
ATT_WIDTH = HEADS * V_DIM
LSTM_WIDTH = HEADS * V_DIM
N_GATES = 4 * HEADS
D_FF = 2816
CHUNK = 128
ALPHA = (2 * DEPTH) ** 0.25
EPS = 1e-5
QK_SCALE = QK_DIM ** -0.5

C_AQ, C_AK, C_AV, C_LQ, C_LK, C_LV, C_LO, C_LG, C_END = 0, 512, 1024, 1536, 1792, 2048, 2560, 3072, 3088

LANES = 128
MXU_TILE = 256
VMEM_LIMIT = 56 * 1024 * 1024

PROJ_BLOCK = 1024
TOKEN_BLOCK = 512
FF_BOUNDS = (0, 6 * MXU_TILE, D_FF)
K_TILE = MXU_TILE
SCORE_BUF_BYTES = 16 * 1024 * 1024


def _attn_sub_blocks(seq):
    return min(SCORE_BUF_BYTES // (2 * K_TILE * seq * 4), seq // K_TILE)

NT_DIMS = (((1,), (1,)), ((), ()))
TN_DIMS = (((0,), (0,)), ((), ()))


def _params(sem):
    return pltpu.CompilerParams(dimension_semantics=sem, vmem_limit_bytes=VMEM_LIMIT)


def _layer_norm(z, g, b):
    mu = jnp.mean(z, axis=-1, keepdims=True)
    zc = z - mu
    var = jnp.mean(zc * zc, axis=-1, keepdims=True)
    return zc * lax.rsqrt(var + EPS) * g + b


def _inproj_kernel(x_ref, w_ref, wakt_ref, wkt_ref, wgt_ref, gb_ref,
                   aq_ref, akt_ref, av_ref, lq_ref, lkt_ref, lv_ref, lo_ref, lgt_ref):
    xb = x_ref[...].astype(BF16)

    def seg(a, b):
        return jnp.dot(xb, w_ref[:, a:b], preferred_element_type=F32)

    aq_ref[...] = (seg(C_AQ, C_AK) * QK_SCALE).astype(BF16)
    akt_ref[...] = lax.dot_general(wakt_ref[...], xb, NT_DIMS, preferred_element_type=F32).astype(BF16)
    av_ref[...] = seg(C_AV, C_LQ).astype(BF16)
    lq_ref[...] = seg(C_LQ, C_LK).astype(BF16)
    kt = lax.dot_general(wkt_ref[...], xb, NT_DIMS, preferred_element_type=F32)
    lkt_ref[...] = (kt * QK_SCALE).astype(BF16)
    lv_ref[...] = seg(C_LV, C_LO).astype(BF16)
    lo_ref[...] = seg(C_LO, C_LG).astype(BF16)
    g = lax.dot_general(wgt_ref[...], xb, NT_DIMS, preferred_element_type=F32) + gb_ref[...]
    row = lax.broadcasted_iota(jnp.int32, g.shape, 0)
    is_forget = ((row >> 2) & 1) == 1
    log_sig = jnp.minimum(g, 0.0) - jnp.log1p(jnp.exp(-jnp.abs(g)))
    lgt_ref[...] = jnp.where(is_forget, log_sig, g)


def _inproj(x2d, w_main, w_akt, w_kt, w_gt, gbias, tm):
    t = x2d.shape[0]
    grid = (t // tm,)
    row_blk = lambda n: pl.BlockSpec((tm, n), lambda i: (i, 0))
    full = lambda a: pl.BlockSpec(a.shape, lambda i: (0, 0))
    out_shapes = (
        jax.ShapeDtypeStruct((t, 512), BF16),
        jax.ShapeDtypeStruct((512, t), BF16),
        jax.ShapeDtypeStruct((t, ATT_WIDTH), BF16),
        jax.ShapeDtypeStruct((t, 256), BF16),
        jax.ShapeDtypeStruct((256, t), BF16),
        jax.ShapeDtypeStruct((t, LSTM_WIDTH), BF16),
        jax.ShapeDtypeStruct((t, LSTM_WIDTH), BF16),
        jax.ShapeDtypeStruct((N_GATES, t), F32),
    )
    out_specs = (
        row_blk(512), pl.BlockSpec((512, tm), lambda i: (0, i)), row_blk(ATT_WIDTH), row_blk(256),
        pl.BlockSpec((256, tm), lambda i: (0, i)),
        row_blk(LSTM_WIDTH), row_blk(LSTM_WIDTH),
        pl.BlockSpec((N_GATES, tm), lambda i: (0, i)),
    )
    return pl.pallas_call(
        _inproj_kernel,
        grid=grid,
        in_specs=[row_blk(D_MODEL), full(w_main), full(w_akt), full(w_kt), full(w_gt), full(gbias)],
        out_specs=out_specs,
        out_shape=out_shapes,
        compiler_params=_params(("parallel",)),
        name="inproj",
    )(x2d, w_main, w_akt, w_kt, w_gt, gbias)


def _alibi_slope(h):
    return jnp.where(h == 0, 0.25, jnp.where(h == 1, 0.0625, jnp.where(h == 2, 0.015625, 0.00390625))).astype(F32)


def _attn_kernel(q_ref, k_ref, v_ref, lq1_ref, lk1_ref, lq2_ref, lk2_ref, g_ref, o_ref,
                 kcat_ref, vaug_ref, qv_ref, sa_ref, sb_ref, *, seq, n_blocks, lam_init):
    SUB = _attn_sub_blocks(seq)
    Q_BLOCK = SUB * K_TILE
    n_tiles = seq // K_TILE
    nq = seq // Q_BLOCK
    step = pl.program_id(0)
    blk = jnp.minimum(step, n_blocks - 1)
    head_idx = blk // nq
    h = head_idx % HEADS
    qi = blk % nq
    prev_head_idx = jnp.maximum(step - 1, 0) // nq
    slope = _alibi_slope(h)

    @pl.when(step == 0)
    def _init_constants():
        sb_ref[...] = jnp.zeros_like(sb_ref)
        frow = lax.broadcasted_iota(jnp.int32, (LANES, seq), 0)
        j = lax.broadcasted_iota(jnp.int32, (LANES, seq), 1)
        kpos = jnp.where(frow < 2, 1.0,
                         jnp.where(frow == 2, (j >> 6).astype(F32),
                                   jnp.where(frow == 3, (j & 63).astype(F32), 0.0))).astype(BF16)
        zeros = jnp.zeros((QK_DIM, K_TILE), BF16)
        for t in range(n_tiles):
            cols = slice(t * K_TILE, (t + 1) * K_TILE)
            kcat_ref[t, QK_DIM:LANES, 0:K_TILE] = zeros
            kcat_ref[t, 0:QK_DIM, K_TILE:2 * K_TILE] = zeros
            kcat_ref[t, LANES:2 * LANES, 0:K_TILE] = kpos[:, cols]
            kcat_ref[t, LANES:2 * LANES, K_TILE:2 * K_TILE] = kpos[:, cols]
        for slot in range(2):
            vaug_ref[slot, :, V_DIM:2 * V_DIM] = jnp.ones((seq, V_DIM), BF16)

    @pl.when((qi == 0) & (step < n_blocks))
    def _load_keys():
        for t in range(n_tiles):
            cols = slice(t * K_TILE, (t + 1) * K_TILE)
            kcat_ref[t, 0:QK_DIM, 0:K_TILE] = k_ref[0:QK_DIM, cols]
            kcat_ref[t, QK_DIM:LANES, K_TILE:2 * K_TILE] = k_ref[QK_DIM:LANES, cols]
        vaug_ref[head_idx % 2, :, 0:V_DIM] = v_ref[...]

    def block_step(s_new, s_old):
        q = q_ref[...]
        lane = lax.broadcasted_iota(jnp.int32, (Q_BLOCK, LANES), 1)
        row = lax.broadcasted_iota(jnp.int32, (Q_BLOCK, LANES), 0)
        i = qi * Q_BLOCK + row
        feat = slope * jnp.where(lane == 0, -64.0 * (i >> 6).astype(F32),
                                 jnp.where(lane == 1, -(i & 63).astype(F32),
                                           jnp.where(lane == 2, 64.0, jnp.where(lane == 3, 1.0, 0.0))))
        sub = row // K_TILE
        for var in range(SUB + 2):
            if var == 0:
                coef = 1.0
            elif var == SUB + 1:
                coef = -1.0
            else:
                coef = jnp.where(sub > var - 1, 1.0, jnp.where(sub == var - 1, 0.0, -1.0))
            qv_ref[var, :, 0:LANES] = q
            qv_ref[var, :, LANES:2 * LANES] = (coef * feat).astype(BF16)

        vslot = prev_head_idx % 2
        row_max = []
        for m in range(2):
            mx = s_old[m, 0]
            for t in range(1, n_tiles):
                mx = jnp.maximum(mx, s_old[m, t])
            row_max.append(jnp.broadcast_to(jnp.max(mx, axis=1, keepdims=True), (Q_BLOCK, K_TILE)))

        acc = jnp.zeros((2 * Q_BLOCK, 2 * V_DIM), F32)
        for t in range(n_tiles):
            p = jnp.concatenate([jnp.exp(s_old[m, t] - row_max[m]).astype(BF16) for m in range(2)], axis=0)
            acc = acc + jnp.dot(p, vaug_ref[vslot, t * K_TILE:(t + 1) * K_TILE, :],
                                preferred_element_type=F32)
            sel = jnp.clip(t - SUB * qi + 1, 0, SUB + 1)
            s = jnp.dot(qv_ref[sel], kcat_ref[t], preferred_element_type=F32)
            s_new[0, t] = s[:, 0:K_TILE]
            s_new[1, t] = s[:, K_TILE:2 * K_TILE]
        r = lax.broadcasted_iota(jnp.int32, (K_TILE, K_TILE), 0)
        c = lax.broadcasted_iota(jnp.int32, (K_TILE, K_TILE), 1)
        diag_bias = -slope * jnp.abs(r - c).astype(F32)
        for sb in range(SUB):
            rows = slice(sb * K_TILE, (sb + 1) * K_TILE)
            for m in range(2):
                s_new[m, SUB * qi + sb, rows, :] = s_new[m, SUB * qi + sb, rows, :] + diag_bias

        outs = [acc[m * Q_BLOCK:(m + 1) * Q_BLOCK, 0:V_DIM] / acc[m * Q_BLOCK:(m + 1) * Q_BLOCK, V_DIM:2 * V_DIM]
                for m in range(2)]
        lam = (jnp.exp(jnp.sum(lq1_ref[...] * lk1_ref[...], keepdims=True))
               - jnp.exp(jnp.sum(lq2_ref[...] * lk2_ref[...], keepdims=True)) + lam_init)
        att = outs[0] - lam * outs[1]
        att = att * lax.rsqrt(jnp.mean(att * att, axis=-1, keepdims=True) + EPS)
        o_ref[...] = (att * g_ref[...] * (1.0 - lam_init)).astype(o_ref.dtype)

    @pl.when(step % 2 == 0)
    def _even():
        block_step(sa_ref, sb_ref)

    @pl.when(step % 2 == 1)
    def _odd():
        block_step(sb_ref, sa_ref)


def _attention(aq, akt, av, lq1, lk1, lq2, lk2, att_g, batch, seq, lam_init):
    t = batch * seq
    SUB = _attn_sub_blocks(seq)
    Q_BLOCK = SUB * K_TILE
    nq = seq // Q_BLOCK
    n_tiles = seq // K_TILE
    n_blocks = batch * HEADS * nq

    def cur(i):
        blk = jnp.minimum(i, n_blocks - 1)
        return blk // (HEADS * nq), (blk // nq) % HEADS, blk % nq

    def prev(i):
        blk = jnp.maximum(i - 1, 0)
        return blk // (HEADS * nq), (blk // nq) % HEADS, blk % nq

    def q_map(i):
        b, h, q = cur(i)
        return b * nq + q, h

    def kt_map(i):
        b, h, _ = cur(i)
        return h, b

    def v_map(i):
        b, h, _ = cur(i)
        return b, h

    def out_map(i):
        b, h, q = prev(i)
        return b * nq + q, h

    small = pl.BlockSpec((1, QK_DIM), lambda i: (0, 0))
    kernel = functools.partial(_attn_kernel, seq=seq, n_blocks=n_blocks, lam_init=lam_init)
    score_buf = pltpu.VMEM((2, n_tiles, Q_BLOCK, K_TILE), F32)
    return pl.pallas_call(
        kernel,
        grid=(n_blocks + 1,),
        in_specs=[
            pl.BlockSpec((Q_BLOCK, LANES), q_map),
            pl.BlockSpec((LANES, seq), kt_map),
            pl.BlockSpec((seq, V_DIM), v_map),
            small, small, small, small,
            pl.BlockSpec((1, V_DIM), lambda i: (0, prev(i)[1])),
        ],
        out_specs=pl.BlockSpec((Q_BLOCK, V_DIM), out_map),
        out_shape=jax.ShapeDtypeStruct((t, ATT_WIDTH), BF16),
        scratch_shapes=[
            pltpu.VMEM((n_tiles, 2 * LANES, 2 * K_TILE), BF16),
            pltpu.VMEM((2, seq, 2 * V_DIM), BF16),
            pltpu.VMEM((SUB + 2, Q_BLOCK, 2 * LANES), BF16),
            score_buf, score_buf,
        ],
        compiler_params=_params(("arbitrary",)),
        name="diff_attention",
    )(aq, akt, av, lq1, lk1, lq2, lk2, att_g)


def _split3(x):
    hi = x.astype(BF16).astype(F32)
    r1 = x - hi
    mid = r1.astype(BF16).astype(F32)
    lo = (r1 - mid).astype(BF16).astype(F32)
    return hi, mid, lo


def _chunk_scan(x, pos, combine, fill, reverse, seq):
    k = 1
    while k < CHUNK:
        if reverse:
            shifted = pltpu.roll(x, seq - k, 1)
            ok = pos < CHUNK - k
        else:
            shifted = pltpu.roll(x, k, 1)
            ok = pos >= k
        x = combine(x, jnp.where(ok, shifted, fill))
        k *= 2
    return x


def _mlstm_kernel(q_ref, kt_ref, v_ref, o_ref_in, gt_ref, g_ref, out_ref,
                  u_ref, bs_ref, cm_ref, hacc_ref, kz_ref, c_ref, cext_ref, lhs_ref, rhs_ref, *, seq):
    L = CHUNK
    nc = seq // L

    gts = gt_ref[...]
    row = lax.broadcasted_iota(jnp.int32, gts.shape, 0)
    pos = lax.broadcasted_iota(jnp.int32, gts.shape, 1) & (L - 1)
    psum = _chunk_scan(gts, pos, jnp.add, 0.0, False, seq)
    ssum = _chunk_scan(gts, pos, jnp.add, 0.0, True, seq)
    bsum = jnp.where(row < 2 * HEADS, psum, ssum)
    u = gts - pltpu.roll(bsum, N_GATES - HEADS, 0)
    cmax = jnp.where(row < 2 * HEADS,
                     _chunk_scan(u, pos, jnp.maximum, -jnp.inf, False, seq),
                     _chunk_scan(u, pos, jnp.maximum, -jnp.inf, True, seq))
    u_ref[...] = u
    bs_ref[...] = bsum
    cm_ref[...] = cmax

    hacc_ref[...] = jnp.zeros_like(hacc_ref)
    kz_ref[...] = jnp.zeros_like(kz_ref)
    c_ref[...] = jnp.zeros_like(c_ref)
    cext_ref[...] = jnp.zeros_like(cext_ref)
    lrow = lax.broadcasted_iota(jnp.int32, (16, L), 0)
    lhs_const = jnp.where(lrow < 3, 1.0, 0.0).astype(F32)
    rrow = lax.broadcasted_iota(jnp.int32, (16, 3 * L), 0)
    rcol = lax.broadcasted_iota(jnp.int32, (16, 3 * L), 1) >> 7
    rhs_const = jnp.where(((rrow >= 3) & (rrow < 6)) | ((rrow >= 6) & (rrow < 9) & (rcol == 2)),
                          1.0, 0.0).astype(F32)
    for n in range(2 * HEADS):
        lhs_ref[n] = lhs_const
        rhs_ref[n] = rhs_const

    jj = lax.broadcasted_iota(jnp.int32, (L, L), 0)
    ss = lax.broadcasted_iota(jnp.int32, (L, L), 1)
    masks = (ss <= jj, ss >= jj)
    ones_v = jnp.ones((L, V_DIM), BF16)

    def chunk_matmuls(d, h, c, m_prev):
        n = d * HEADS + h
        c0 = pl.multiple_of(c * L, L)
        cols = pl.ds(c0, L)
        hrows = slice(h * QK_DIM, (h + 1) * QK_DIM)
        u_row = u_ref[8 * d + h:8 * d + h + 1, cols]
        b_row = bs_ref[8 * d + HEADS + h:8 * d + HEADS + h + 1, cols]
        cm_row = cm_ref[8 * d + h:8 * d + h + 1, cols]
        umax = jnp.max(u_row, axis=1, keepdims=True)
        g_tot = b_row[:, L - 1:L] if d == 0 else b_row[:, 0:1]
        mx_row = jnp.maximum(m_prev, cm_row)

        for r0, piece in ((3, -mx_row), (6, -b_row)):
            for k, part in enumerate(_split3(piece)):
                lhs_ref[n, r0 + k:r0 + k + 1, :] = part
        for k, part in enumerate(_split3(u_row)):
            rhs_ref[n, k:k + 1, 0:L] = part
        for k, part in enumerate(_split3(m_prev)):
            rhs_ref[n, k:k + 1, L:2 * L] = jnp.broadcast_to(part, (1, L))
        outer = lax.dot_general(lhs_ref[n].astype(BF16), rhs_ref[n].astype(BF16), TN_DIMS,
                                preferred_element_type=F32)
        q = q_ref[cols, :]
        kh = kt_ref[hrows, cols]
        kz_ref[n, hrows, :] = kh
        qk = jnp.dot(q, kz_ref[n], preferred_element_type=F32)
        inter = jnp.dot(q, cext_ref[n], preferred_element_type=F32)
        return (d, h, cols, u_row, umax, g_tot, m_prev, outer, kh, qk, inter)

    def chunk_finish(stage1):
        d, h, cols, u_row, umax, g_tot, m_prev, outer, kh, qk, inter = stage1
        n = d * HEADS + h
        hrows = slice(h * QK_DIM, (h + 1) * QK_DIM)
        hcols = slice(h * V_DIM, (h + 1) * V_DIM)
        e = jnp.exp(outer)
        p = jnp.where(masks[d], e[:, 0:L], 0.0)
        e_inter = e[:, L:2 * L]
        clamp = e[:, 2 * L:3 * L]
        sqk = (qk * p).astype(BF16)
        vaug = jnp.concatenate([v_ref[cols, hcols], ones_v], axis=1)
        ke = (kh.astype(F32) * jnp.exp(u_row - umax)).astype(BF16)
        both = jnp.dot(jnp.concatenate([sqk, ke], axis=0), vaug, preferred_element_type=F32)
        intra = both[0:L]
        c_loc = both[L:L + QK_DIM]
        nd = intra + jnp.concatenate([e_inter, e_inter], axis=1) * inter
        hout = nd[:, 0:V_DIM] / jnp.maximum(jnp.abs(nd[:, V_DIM:2 * V_DIM]), clamp)
        hacc_ref[cols, hcols] = hacc_ref[cols, hcols] + hout

        m_loc = g_tot + umax
        m_new = jnp.maximum(g_tot + m_prev, m_loc)
        a = jnp.exp(g_tot + m_prev - m_new)
        bb = jnp.exp(m_loc - m_new)
        c_new = a * c_ref[n] + bb * c_loc
        c_ref[n] = c_new
        cext_ref[n, hrows, :] = c_new.astype(BF16)
        return m_new

    def body(it, carry):
        stage1 = []
        for d in range(2):
            c = it if d == 0 else nc - 1 - it
            for h in range(HEADS):
                stage1.append(chunk_matmuls(d, h, c, carry[d * HEADS + h]))
        return tuple(chunk_finish(s1) for s1 in stage1)

    zero = jnp.zeros((1, 1), F32)
    lax.fori_loop(0, nc, body, (zero,) * (2 * HEADS), unroll=4)

    for h in range(HEADS):
        hcols = slice(h * V_DIM, (h + 1) * V_DIM)
        hs = hacc_ref[:, hcols]
        hn = hs * lax.rsqrt(jnp.mean(hs * hs, axis=-1, keepdims=True) + EPS) * g_ref[:, hcols]
        out_ref[:, hcols] = (jax.nn.sigmoid(o_ref_in[:, hcols].astype(F32)) * hn).astype(out_ref.dtype)


def _mlstm(lq, lkt, lv, lo, lgt, lstm_g, batch, seq):
    t = batch * seq
    kernel = functools.partial(_mlstm_kernel, seq=seq)
    L = CHUNK
    n_rec = 2 * HEADS
    gate_rows = pltpu.VMEM((N_GATES, seq), F32)
    return pl.pallas_call(
        kernel,
        grid=(batch,),
        in_specs=[
            pl.BlockSpec((seq, 256), lambda b: (b, 0)),
            pl.BlockSpec((256, seq), lambda b: (0, b)),
            pl.BlockSpec((seq, LSTM_WIDTH), lambda b: (b, 0)),
            pl.BlockSpec((seq, LSTM_WIDTH), lambda b: (b, 0)),
            pl.BlockSpec((N_GATES, seq), lambda b: (0, b)),
            pl.BlockSpec((1, LSTM_WIDTH), lambda b: (0, 0)),
        ],
        out_specs=pl.BlockSpec((seq, LSTM_WIDTH), lambda b: (b, 0)),
        out_shape=jax.ShapeDtypeStruct((t, LSTM_WIDTH), BF16),
        scratch_shapes=[
            gate_rows,
            gate_rows,
            gate_rows,
            pltpu.VMEM((seq, LSTM_WIDTH), F32),
            pltpu.VMEM((n_rec, 256, L), BF16),
            pltpu.VMEM((n_rec, QK_DIM, 2 * V_DIM), F32),
            pltpu.VMEM((n_rec, 256, 2 * V_DIM), BF16),
            pltpu.VMEM((n_rec, 16, L), F32),
            pltpu.VMEM((n_rec, 16, 3 * L), F32),
        ],
        compiler_params=_params(("parallel",)),
        name="mlstm",
    )(lq, lkt, lv, lo, lgt, lstm_g)


def _outproj_kernel(att_ref, lstm_ref, x_ref, w_ref, g_ref, b_ref, o_ref):
    y = jnp.dot(att_ref[...], w_ref[0:ATT_WIDTH, :], preferred_element_type=F32)
    y = y + jnp.dot(lstm_ref[...], w_ref[ATT_WIDTH:ATT_WIDTH + LSTM_WIDTH, :], preferred_element_type=F32)
    o_ref[...] = _layer_norm(ALPHA * x_ref[...] + y, g_ref[...], b_ref[...])


def _outproj(att, lstm, x2d, w_out, g, b, tm):
    t = x2d.shape[0]
    row_blk = lambda n: pl.BlockSpec((tm, n), lambda i: (i, 0))
    full = lambda a: pl.BlockSpec(a.shape, lambda i: (0, 0))
    return pl.pallas_call(
        _outproj_kernel,
        grid=(t // tm,),
        in_specs=[row_blk(ATT_WIDTH), row_blk(LSTM_WIDTH), row_blk(D_MODEL), full(w_out), full(g), full(b)],
        out_specs=row_blk(D_MODEL),
        out_shape=jax.ShapeDtypeStruct((t, D_MODEL), F32),
        compiler_params=_params(("parallel",)),
        name="outproj_ln1",
    )(att, lstm, x2d, w_out, g, b)


HALO = 8


def _ffn_kernel(x_ref, xp_ref, xn_ref, a_ref, ap_ref, an_ref, l_ref, lp_ref, ln_ref, wo_ref, g1_ref, b1_ref,
                wg_ref, wu_ref, cw_ref, cb_ref, wd_ref, g_ref, b_ref, o_ref, gs_ref, *, seq, tm):
    i = pl.program_id(0)

    def mix_ln1(xr, ar, lr):
        y = jnp.dot(ar, wo_ref[0:ATT_WIDTH, :], preferred_element_type=F32)
        y = y + jnp.dot(lr, wo_ref[ATT_WIDTH:ATT_WIDTH + LSTM_WIDTH, :], preferred_element_type=F32)
        return _layer_norm(ALPHA * xr + y, g1_ref[...], b1_ref[...])

    x1 = mix_ln1(x_ref[...], a_ref[...], l_ref[...])
    xb = x1.astype(BF16)
    near = lambda p_ref, n_ref: jnp.concatenate(
        [p_ref[...].astype(F32)[HALO:2 * HALO], n_ref[...].astype(F32)[0:HALO]], axis=0).astype(BF16)
    x1h = mix_ln1(jnp.concatenate([xp_ref[...], xn_ref[...]], axis=0), near(ap_ref, an_ref), near(lp_ref, ln_ref))
    has_prev = ((i * tm) % seq != 0).astype(F32)
    has_next = (((i + 1) * tm) % seq != 0).astype(F32)
    hrow = lax.broadcasted_iota(jnp.int32, (2 * HALO, 1), 0)
    halo = (x1h * jnp.where(hrow < HALO, has_prev, has_next)).astype(BF16)
    acc = None
    for j in range(len(FF_BOUNDS) - 1):
        cs = slice(FF_BOUNDS[j], FF_BOUNDS[j + 1])
        gs_ref[HALO:HALO + tm, cs] = jnp.dot(xb, wg_ref[:, cs], preferred_element_type=F32)
        gh = jnp.dot(halo, wg_ref[:, cs], preferred_element_type=F32)
        gs_ref[0:HALO, cs] = gh[0:HALO]
        gs_ref[HALO + tm:2 * HALO + tm, cs] = gh[HALO:2 * HALO]
        conv = (cb_ref[:, cs]
                + gs_ref[HALO - 1:HALO - 1 + tm, cs] * cw_ref[0:1, cs]
                + gs_ref[HALO:HALO + tm, cs] * cw_ref[1:2, cs]
                + gs_ref[HALO + 1:HALO + 1 + tm, cs] * cw_ref[2:3, cs])
        up = jnp.dot(xb, wu_ref[:, cs], preferred_element_type=F32)
        gelu = 0.5 * conv * (lax.erf(conv * (1.0 / math.sqrt(2.0))) + 1.0)
        hmid = (gelu * up).astype(BF16)
        part = jnp.dot(hmid, wd_ref[cs, :], preferred_element_type=F32)
        acc = part if acc is None else acc + part
    o_ref[...] = _layer_norm(ALPHA * x1 + acc, g_ref[...], b_ref[...])


def _ffn(x2d, att, lstm, w_out, g1, b1, w_g, w_u, conv_w, conv_b, w_d, g, b, seq, tm):
    t = x2d.shape[0]
    hb = tm // HALO
    last_hblk = t // HALO - 1
    hb2 = tm // (2 * HALO)
    last_hblk2 = t // (2 * HALO) - 1
    prev_rows = lambda rows, width, per, last: pl.BlockSpec((rows, width), lambda i: (jnp.maximum(i * per - 1, 0), 0))
    next_rows = lambda rows, width, per, last: pl.BlockSpec((rows, width),
                                                            lambda i: (jnp.minimum((i + 1) * per, last), 0))
    kernel = functools.partial(_ffn_kernel, seq=seq, tm=tm)
    resident = lambda a: pl.BlockSpec(a.shape, lambda i: (0, 0), pipeline_mode=pl.Buffered(1))
    return pl.pallas_call(
        kernel,
        grid=(t // tm,),
        in_specs=[
            pl.BlockSpec((tm, D_MODEL), lambda i: (i, 0)),
            prev_rows(HALO, D_MODEL, hb, last_hblk), next_rows(HALO, D_MODEL, hb, last_hblk),
            pl.BlockSpec((tm, ATT_WIDTH), lambda i: (i, 0)),
            prev_rows(2 * HALO, ATT_WIDTH, hb2, last_hblk2), next_rows(2 * HALO, ATT_WIDTH, hb2, last_hblk2),
            pl.BlockSpec((tm, LSTM_WIDTH), lambda i: (i, 0)),
            prev_rows(2 * HALO, LSTM_WIDTH, hb2, last_hblk2), next_rows(2 * HALO, LSTM_WIDTH, hb2, last_hblk2),
            resident(w_out), resident(g1), resident(b1),
            resident(w_g), resident(w_u), resident(conv_w), resident(conv_b), resident(w_d),
            resident(g), resident(b),
        ],
        out_specs=pl.BlockSpec((tm, D_MODEL), lambda i: (i, 0)),
        out_shape=jax.ShapeDtypeStruct((t, D_MODEL), F32),
        scratch_shapes=[
            pltpu.VMEM((tm + 2 * HALO, D_FF), F32),
        ],
        compiler_params=_params(("parallel",)),
        name="outproj_ffn",
    )(x2d, x2d, x2d, att, att, att, lstm, lstm, lstm, w_out, g1, b1, w_g, w_u, conv_w, conv_b, w_d, g, b)


def _prep_layer(l, w_in, gate_bias, lam_q1, lam_k1, lam_q2, lam_k2, att_norm_g, lstm_norm_g, w_out,
                ln1_g, ln1_b, w_gu, conv_w, conv_b, w_down, ln2_g, ln2_b):
    wl = w_in[l]
    row = lambda a: a.reshape(1, -1).astype(F32)
    return dict(
        w_main=wl[:, :C_LG].astype(BF16),
        w_akt=wl[:, C_AK:C_AV].T.astype(BF16),
        w_kt=wl[:, C_LK:C_LV].T.astype(BF16),
        w_gt=wl[:, C_LG:C_END].T.astype(BF16),
        gbias=gate_bias[l].reshape(N_GATES, 1).astype(F32),
        lq1=row(lam_q1[l]), lk1=row(lam_k1[l]), lq2=row(lam_q2[l]), lk2=row(lam_k2[l]),
        att_g=row(att_norm_g[l]), lstm_g=row(lstm_norm_g[l]),
        w_out=w_out[l].astype(BF16), ln1_g=row(ln1_g[l]), ln1_b=row(ln1_b[l]),
        w_g=w_gu[l][:, :D_FF].astype(BF16), w_u=w_gu[l][:, D_FF:].astype(BF16),
        conv_w=conv_w[l].astype(F32), conv_b=row(conv_b[l]),
        w_d=w_down[l].astype(BF16), ln2_g=row(ln2_g[l]), ln2_b=row(ln2_b[l]),
    )


def _trunk(x, layers, tm_proj=PROJ_BLOCK, tm=TOKEN_BLOCK):
    batch, seq, _ = x.shape
    x2d = x.reshape(batch * seq, D_MODEL)
    for l, p in enumerate(layers):
        lam_init = 0.8 - 0.6 * math.exp(-0.3 * l)
        aq, akt, av, lq, lkt, lv, lo, lgt = _inproj(x2d, p["w_main"], p["w_akt"], p["w_kt"], p["w_gt"],
                                                    p["gbias"], tm_proj)
        att = _attention(aq, akt, av, p["lq1"], p["lk1"], p["lq2"], p["lk2"], p["att_g"], batch, seq, lam_init)
        lstm = _mlstm(lq, lkt, lv, lo, lgt, p["lstm_g"], batch, seq)
        x2d = _ffn(x2d, att, lstm, p["w_out"], p["ln1_g"], p["ln1_b"], p["w_g"], p["w_u"], p["conv_w"], p["conv_b"], p["w_d"], p["ln2_g"], p["ln2_b"], seq, tm)
    return x2d.reshape(batch, seq, D_MODEL)


def kernel(x_prompt, x_sample, w_in, gate_bias, lam_q1, lam_k1, lam_q2, lam_k2, att_norm_g, lstm_norm_g,
           w_out, ln1_g, ln1_b, w_gu, conv_w, conv_b, w_down, ln2_g, ln2_b):
    layers = [_prep_layer(l, w_in, gate_bias, lam_q1, lam_k1, lam_q2, lam_k2, att_norm_g, lstm_norm_g,
                          w_out, ln1_g, ln1_b, w_gu, conv_w, conv_b, w_down, ln2_g, ln2_b)
              for l in range(DEPTH)]
    return (_trunk(x_prompt, layers), _trunk(x_sample, layers))
```

```python
import functools
import math

import jax
import jax.numpy as jnp
from jax import lax
from jax.experimental import pallas as pl
from jax.experimental.pallas import tpu as pltpu

F32 = jnp.float32
BF16 = jnp.bfloat16

D_MODEL = 1024
DEPTH = 2
HEADS = 4
QK_DIM = 64
V_DIM = 128
ATT_WIDTH = HEADS * V_DIM
LSTM_WIDTH = HEADS * V_DIM
N_GATES = 4 * HEADS
D_FF = 2816
CHUNK = 128
ALPHA = (2 * DEPTH) ** 0.25
EPS = 1e-5
QK_SCALE = QK_DIM ** -0.5

C_AQ, C_AK, C_AV, C_LQ, C_LK, C_LV, C_LO, C_LG, C_END = 0, 512, 1024, 1536, 1792, 2048, 2560, 3072, 3088

LANES = 128
MXU_TILE = 256
VMEM_LIMIT = 60 * 1024 * 1024

PROJ_BLOCK = 1024
TOKEN_BLOCK = 1024
FF_BOUNDS = (0, 6 * MXU_TILE, D_FF)
K_TILE = MXU_TILE
SCORE_BUF_BYTES = 16 * 1024 * 1024


def _attn_sub_blocks(seq):
    return min(SCORE_BUF_BYTES // (2 * K_TILE * seq * 4), seq // K_TILE)

NT_DIMS = (((1,), (1,)), ((), ()))
TN_DIMS = (((0,), (0,)), ((), ()))


def _params(sem):
    return pltpu.CompilerParams(dimension_semantics=sem, vmem_limit_bytes=VMEM_LIMIT)


def _layer_norm(z, g, b):
    mu = jnp.mean(z, axis=-1, keepdims=True)
    zc = z - mu
    var = jnp.mean(zc * zc, axis=-1, keepdims=True)
    return zc * lax.rsqrt(var + EPS) * g + b


def _inproj_kernel(x_ref, w_ref, wakt_ref, wkt_ref, wgt_ref, gb_ref,
                   aq_ref, akt_ref, av_ref, lq_ref, lkt_ref, lv_ref, lo_ref, lgt_ref):
    xb = x_ref[...].astype(BF16)

    def seg(a, b):
        return jnp.dot(xb, w_ref[:, a:b], preferred_element_type=F32)

    aq_ref[...] = (seg(C_AQ, C_AK) * QK_SCALE).astype(BF16)
    akt_ref[...] = lax.dot_general(wakt_ref[...], xb, NT_DIMS, preferred_element_type=F32).astype(BF16)
    av_ref[...] = seg(C_AV, C_LQ).astype(BF16)
    lq_ref[...] = seg(C_LQ, C_LK).astype(BF16)
    kt = lax.dot_general(wkt_ref[...], xb, NT_DIMS, preferred_element_type=F32)
    lkt_ref[...] = (kt * QK_SCALE).astype(BF16)
    lv_ref[...] = seg(C_LV, C_LO).astype(BF16)
    lo_ref[...] = seg(C_LO, C_LG).astype(BF16)
    g = lax.dot_general(wgt_ref[...], xb, NT_DIMS, preferred_element_type=F32) + gb_ref[...]
    row = lax.broadcasted_iota(jnp.int32, g.shape, 0)
    is_forget = ((row >> 2) & 1) == 1
    log_sig = jnp.minimum(g, 0.0) - jnp.log1p(jnp.exp(-jnp.abs(g)))
    lgt_ref[...] = jnp.where(is_forget, log_sig, g)


def _inproj(x2d, w_main, w_akt, w_kt, w_gt, gbias, tm):
    t = x2d.shape[0]
    grid = (t // tm,)
    row_blk = lambda n: pl.BlockSpec((tm, n), lambda i: (i, 0))
    full = lambda a: pl.BlockSpec(a.shape, lambda i: (0, 0))
    out_shapes = (
        jax.ShapeDtypeStruct((t, 512), BF16),
        jax.ShapeDtypeStruct((512, t), BF16),
        jax.ShapeDtypeStruct((t, ATT_WIDTH), BF16),
        jax.ShapeDtypeStruct((t, 256), BF16),
        jax.ShapeDtypeStruct((256, t), BF16),
        jax.ShapeDtypeStruct((t, LSTM_WIDTH), BF16),
        jax.ShapeDtypeStruct((t, LSTM_WIDTH), BF16),
        jax.ShapeDtypeStruct((N_GATES, t), F32),
    )
    out_specs = (
        row_blk(512), pl.BlockSpec((512, tm), lambda i: (0, i)), row_blk(ATT_WIDTH), row_blk(256),
        pl.BlockSpec((256, tm), lambda i: (0, i)),
        row_blk(LSTM_WIDTH), row_blk(LSTM_WIDTH),
        pl.BlockSpec((N_GATES, tm), lambda i: (0, i)),
    )
    return pl.pallas_call(
        _inproj_kernel,
        grid=grid,
        in_specs=[row_blk(D_MODEL), full(w_main), full(w_akt), full(w_kt), full(w_gt), full(gbias)],
        out_specs=out_specs,
        out_shape=out_shapes,
        compiler_params=_params(("parallel",)),
        name="inproj",
    )(x2d, w_main, w_akt, w_kt, w_gt, gbias)


def _alibi_slope(h):
    return jnp.where(h == 0, 0.25, jnp.where(h == 1, 0.0625, jnp.where(h == 2, 0.015625, 0.00390625))).astype(F32)


def _attn_kernel(q_ref, k_ref, v_ref, lq1_ref, lk1_ref, lq2_ref, lk2_ref, g_ref, o_ref,
                 kcat_ref, vaug_ref, qv_ref, sa_ref, sb_ref, *, seq, n_blocks, lam_init):
    SUB = _attn_sub_blocks(seq)
    Q_BLOCK = SUB * K_TILE
    n_tiles = seq // K_TILE
    nq = seq // Q_BLOCK
    step = pl.program_id(0)
    blk = jnp.minimum(step, n_blocks - 1)
    head_idx = blk // nq
    h = head_idx % HEADS
    qi = blk % nq
    prev_head_idx = jnp.maximum(step - 1, 0) // nq
    slope = _alibi_slope(h)

    @pl.when(step == 0)
    def _init_constants():
        sb_ref[...] = jnp.zeros_like(sb_ref)
        frow = lax.broadcasted_iota(jnp.int32, (LANES, seq), 0)
        j = lax.broadcasted_iota(jnp.int32, (LANES, seq), 1)
        kpos = jnp.where(frow < 2, 1.0,
                         jnp.where(frow == 2, (j >> 6).astype(F32),
                                   jnp.where(frow == 3, (j & 63).astype(F32), 0.0))).astype(BF16)
        zeros = jnp.zeros((QK_DIM, K_TILE), BF16)
        for t in range(n_tiles):
            cols = slice(t * K_TILE, (t + 1) * K_TILE)
            kcat_ref[t, QK_DIM:LANES, 0:K_TILE] = zeros
            kcat_ref[t, 0:QK_DIM, K_TILE:2 * K_TILE] = zeros
            kcat_ref[t, LANES:2 * LANES, 0:K_TILE] = kpos[:, cols]
            kcat_ref[t, LANES:2 * LANES, K_TILE:2 * K_TILE] = kpos[:, cols]
        for slot in range(2):
            vaug_ref[slot, :, V_DIM:2 * V_DIM] = jnp.ones((seq, V_DIM), BF16)

    @pl.when((qi == 0) & (step < n_blocks))
    def _load_keys():
        for t in range(n_tiles):
            cols = slice(t * K_TILE, (t + 1) * K_TILE)
            kcat_ref[t, 0:QK_DIM, 0:K_TILE] = k_ref[0:QK_DIM, cols]
            kcat_ref[t, QK_DIM:LANES, K_TILE:2 * K_TILE] = k_ref[QK_DIM:LANES, cols]
        vaug_ref[head_idx % 2, :, 0:V_DIM] = v_ref[...]

    def block_step(s_new, s_old):
        q = q_ref[...]
        lane = lax.broadcasted_iota(jnp.int32, (Q_BLOCK, LANES), 1)
        row = lax.broadcasted_iota(jnp.int32, (Q_BLOCK, LANES), 0)
        i = qi * Q_BLOCK + row
        feat = slope * jnp.where(lane == 0, -64.0 * (i >> 6).astype(F32),
                                 jnp.where(lane == 1, -(i & 63).astype(F32),
                                           jnp.where(lane == 2, 64.0, jnp.where(lane == 3, 1.0, 0.0))))
        sub = row // K_TILE
        for var in range(SUB + 2):
            if var == 0:
                coef = 1.0
            elif var == SUB + 1:
                coef = -1.0
            else:
                coef = jnp.where(sub > var - 1, 1.0, jnp.where(sub == var - 1, 0.0, -1.0))
            qv_ref[var, :, 0:LANES] = q
            qv_ref[var, :, LANES:2 * LANES] = (coef * feat).astype(BF16)

        vslot = prev_head_idx % 2
        row_max = []
        for m in range(2):
            mx = s_old[m, 0]
            for t in range(1, n_tiles):
                mx = jnp.maximum(mx, s_old[m, t])
            row_max.append(jnp.broadcast_to(jnp.max(mx, axis=1, keepdims=True), (Q_BLOCK, K_TILE)))

        acc = jnp.zeros((2 * Q_BLOCK, 2 * V_DIM), F32)
        for t in range(n_tiles):
            p = jnp.concatenate([jnp.exp(s_old[m, t] - row_max[m]).astype(BF16) for m in range(2)], axis=0)
            acc = acc + jnp.dot(p, vaug_ref[vslot, t * K_TILE:(t + 1) * K_TILE, :],
                                preferred_element_type=F32)
            sel = jnp.clip(t - SUB * qi + 1, 0, SUB + 1)
            s = jnp.dot(qv_ref[sel], kcat_ref[t], preferred_element_type=F32)
            s_new[0, t] = s[:, 0:K_TILE]
            s_new[1, t] = s[:, K_TILE:2 * K_TILE]
        r = lax.broadcasted_iota(jnp.int32, (K_TILE, K_TILE), 0)
        c = lax.broadcasted_iota(jnp.int32, (K_TILE, K_TILE), 1)
        diag_bias = -slope * jnp.abs(r - c).astype(F32)
        for sb in range(SUB):
            rows = slice(sb * K_TILE, (sb + 1) * K_TILE)
            for m in range(2):
                s_new[m, SUB * qi + sb, rows, :] = s_new[m, SUB * qi + sb, rows, :] + diag_bias

        outs = [acc[m * Q_BLOCK:(m + 1) * Q_BLOCK, 0:V_DIM] / acc[m * Q_BLOCK:(m + 1) * Q_BLOCK, V_DIM:2 * V_DIM]
                for m in range(2)]
        lam = (jnp.exp(jnp.sum(lq1_ref[...] * lk1_ref[...], keepdims=True))
               - jnp.exp(jnp.sum(lq2_ref[...] * lk2_ref[...], keepdims=True)) + lam_init)
        att = outs[0] - lam * outs[1]
        att = att * lax.rsqrt(jnp.mean(att * att, axis=-1, keepdims=True) + EPS)
        o_ref[...] = (att * g_ref[...] * (1.0 - lam_init)).astype(o_ref.dtype)

    @pl.when(step % 2 == 0)
    def _even():
        block_step(sa_ref, sb_ref)

    @pl.when(step % 2 == 1)
    def _odd():
        block_step(sb_ref, sa_ref)


def _attention(aq, akt, av, lq1, lk1, lq2, lk2, att_g, batch, seq, lam_init):
    t = batch * seq
    SUB = _attn_sub_blocks(seq)
    Q_BLOCK = SUB * K_TILE
    nq = seq // Q_BLOCK
    n_tiles = seq // K_TILE
    n_blocks = batch * HEADS * nq

    def cur(i):
        blk = jnp.minimum(i, n_blocks - 1)
        return blk // (HEADS * nq), (blk // nq) % HEADS, blk % nq

    def prev(i):
        blk = jnp.maximum(i - 1, 0)
        return blk // (HEADS * nq), (blk // nq) % HEADS, blk % nq

    def q_map(i):
        b, h, q = cur(i)
        return b * nq + q, h

    def kt_map(i):
        b, h, _ = cur(i)
        return h, b

    def v_map(i):
        b, h, _ = cur(i)
        return b, h

    def out_map(i):
        b, h, q = prev(i)
        return b * nq + q, h

    small = pl.BlockSpec((1, QK_DIM), lambda i: (0, 0))
    kernel = functools.partial(_attn_kernel, seq=seq, n_blocks=n_blocks, lam_init=lam_init)
    score_buf = pltpu.VMEM((2, n_tiles, Q_BLOCK, K_TILE), F32)
    return pl.pallas_call(
        kernel,
        grid=(n_blocks + 1,),
        in_specs=[
            pl.BlockSpec((Q_BLOCK, LANES), q_map),
            pl.BlockSpec((LANES, seq), kt_map),
            pl.BlockSpec((seq, V_DIM), v_map),
            small, small, small, small,
            pl.BlockSpec((1, V_DIM), lambda i: (0, prev(i)[1])),
        ],
        out_specs=pl.BlockSpec((Q_BLOCK, V_DIM), out_map),
        out_shape=jax.ShapeDtypeStruct((t, ATT_WIDTH), BF16),
        scratch_shapes=[
            pltpu.VMEM((n_tiles, 2 * LANES, 2 * K_TILE), BF16),
            pltpu.VMEM((2, seq, 2 * V_DIM), BF16),
            pltpu.VMEM((SUB + 2, Q_BLOCK, 2 * LANES), BF16),
            score_buf, score_buf,
        ],
        compiler_params=_params(("arbitrary",)),
        name="diff_attention",
    )(aq, akt, av, lq1, lk1, lq2, lk2, att_g)


def _split3(x):
    hi = x.astype(BF16).astype(F32)
    r1 = x - hi
    mid = r1.astype(BF16).astype(F32)
    lo = (r1 - mid).astype(BF16).astype(F32)
    return hi, mid, lo


def _chunk_scan(x, pos, combine, fill, reverse, seq):
    k = 1
    while k < CHUNK:
        if reverse:
            shifted = pltpu.roll(x, seq - k, 1)
            ok = pos < CHUNK - k
        else:
            shifted = pltpu.roll(x, k, 1)
            ok = pos >= k
        x = combine(x, jnp.where(ok, shifted, fill))
        k *= 2
    return x


def _mlstm_kernel(q_ref, kt_ref, v_ref, o_ref_in, gt_ref, g_ref, out_ref,
                  u_ref, bs_ref, cm_ref, hacc_ref, kz_ref, c_ref, cext_ref, lhs_ref, rhs_ref, *, seq):
    L = CHUNK
    nc = seq // L

    gts = gt_ref[...]
    row = lax.broadcasted_iota(jnp.int32, gts.shape, 0)
    pos = lax.broadcasted_iota(jnp.int32, gts.shape, 1) & (L - 1)
    psum = _chunk_scan(gts, pos, jnp.add, 0.0, False, seq)
    ssum = _chunk_scan(gts, pos, jnp.add, 0.0, True, seq)
    bsum = jnp.where(row < 2 * HEADS, psum, ssum)
    u = gts - pltpu.roll(bsum, N_GATES - HEADS, 0)
    cmax = jnp.where(row < 2 * HEADS,
                     _chunk_scan(u, pos, jnp.maximum, -jnp.inf, False, seq),
                     _chunk_scan(u, pos, jnp.maximum, -jnp.inf, True, seq))
    u_ref[...] = u
    bs_ref[...] = bsum
    cm_ref[...] = cmax

    hacc_ref[...] = jnp.zeros_like(hacc_ref)
    kz_ref[...] = jnp.zeros_like(kz_ref)
    c_ref[...] = jnp.zeros_like(c_ref)
    cext_ref[...] = jnp.zeros_like(cext_ref)
    lrow = lax.broadcasted_iota(jnp.int32, (16, L), 0)
    lhs_const = jnp.where(lrow < 3, 1.0, 0.0).astype(F32)
    rrow = lax.broadcasted_iota(jnp.int32, (16, 3 * L), 0)
    rcol = lax.broadcasted_iota(jnp.int32, (16, 3 * L), 1) >> 7
    rhs_const = jnp.where(((rrow >= 3) & (rrow < 6)) | ((rrow >= 6) & (rrow < 9) & (rcol == 2)),
                          1.0, 0.0).astype(F32)
    for n in range(2 * HEADS):
        lhs_ref[n] = lhs_const
        rhs_ref[n] = rhs_const

    jj = lax.broadcasted_iota(jnp.int32, (L, L), 0)
    ss = lax.broadcasted_iota(jnp.int32, (L, L), 1)
    masks = (ss <= jj, ss >= jj)
    ones_v = jnp.ones((L, V_DIM), BF16)

    def chunk_matmuls(d, h, c, m_prev):
        n = d * HEADS + h
        c0 = pl.multiple_of(c * L, L)
        cols = pl.ds(c0, L)
        hrows = slice(h * QK_DIM, (h + 1) * QK_DIM)
        u_row = u_ref[8 * d + h:8 * d + h + 1, cols]
        b_row = bs_ref[8 * d + HEADS + h:8 * d + HEADS + h + 1, cols]
        cm_row = cm_ref[8 * d + h:8 * d + h + 1, cols]
        umax = jnp.max(u_row, axis=1, keepdims=True)
        g_tot = b_row[:, L - 1:L] if d == 0 else b_row[:, 0:1]
        mx_row = jnp.maximum(m_prev, cm_row)

        for r0, piece in ((3, -mx_row), (6, -b_row)):
            for k, part in enumerate(_split3(piece)):
                lhs_ref[n, r0 + k:r0 + k + 1, :] = part
        for k, part in enumerate(_split3(u_row)):
            rhs_ref[n, k:k + 1, 0:L] = part
        for k, part in enumerate(_split3(m_prev)):
            rhs_ref[n, k:k + 1, L:2 * L] = jnp.broadcast_to(part, (1, L))
        outer = lax.dot_general(lhs_ref[n].astype(BF16), rhs_ref[n].astype(BF16), TN_DIMS,
                                preferred_element_type=F32)
        q = q_ref[cols, :]
        kh = kt_ref[hrows, cols]
        kz_ref[n, hrows, :] = kh
        qk = jnp.dot(q, kz_ref[n], preferred_element_type=F32)
        inter = jnp.dot(q, cext_ref[n], preferred_element_type=F32)
        return (d, h, cols, u_row, umax, g_tot, m_prev, outer, kh, qk, inter)

    def chunk_finish(stage1):
        d, h, cols, u_row, umax, g_tot, m_prev, outer, kh, qk, inter = stage1
        n = d * HEADS + h
        hrows = slice(h * QK_DIM, (h + 1) * QK_DIM)
        hcols = slice(h * V_DIM, (h + 1) * V_DIM)
        e = jnp.exp(outer)
        p = jnp.where(masks[d], e[:, 0:L], 0.0)
        e_inter = e[:, L:2 * L]
        clamp = e[:, 2 * L:3 * L]
        sqk = (qk * p).astype(BF16)
        vaug = jnp.concatenate([v_ref[cols, hcols], ones_v], axis=1)
        ke = (kh.astype(F32) * jnp.exp(u_row - umax)).astype(BF16)
        both = jnp.dot(jnp.concatenate([sqk, ke], axis=0), vaug, preferred_element_type=F32)
        intra = both[0:L]
        c_loc = both[L:L + QK_DIM]
        nd = intra + jnp.concatenate([e_inter, e_inter], axis=1) * inter
        hout = nd[:, 0:V_DIM] / jnp.maximum(jnp.abs(nd[:, V_DIM:2 * V_DIM]), clamp)
        hacc_ref[cols, hcols] = hacc_ref[cols, hcols] + hout

        m_loc = g_tot + umax
        m_new = jnp.maximum(g_tot + m_prev, m_loc)
        a = jnp.exp(g_tot + m_prev - m_new)
        bb = jnp.exp(m_loc - m_new)
        c_new = a * c_ref[n] + bb * c_loc
        c_ref[n] = c_new
        cext_ref[n, hrows, :] = c_new.astype(BF16)
        return m_new

    def body(it, carry):
        stage1 = []
        for d in range(2):
            c = it if d == 0 else nc - 1 - it
            for h in range(HEADS):
                stage1.append(chunk_matmuls(d, h, c, carry[d * HEADS + h]))
        return tuple(chunk_finish(s1) for s1 in stage1)

    zero = jnp.zeros((1, 1), F32)
    lax.fori_loop(0, nc, body, (zero,) * (2 * HEADS), unroll=4)

    for h in range(HEADS):
        hcols = slice(h * V_DIM, (h + 1) * V_DIM)
        hs = hacc_ref[:, hcols]
        hn = hs * lax.rsqrt(jnp.mean(hs * hs, axis=-1, keepdims=True) + EPS) * g_ref[:, hcols]
        out_ref[:, hcols] = (jax.nn.sigmoid(o_ref_in[:, hcols].astype(F32)) * hn).astype(out_ref.dtype)


def _mlstm(lq, lkt, lv, lo, lgt, lstm_g, batch, seq):
    t = batch * seq
    kernel = functools.partial(_mlstm_kernel, seq=seq)
    L = CHUNK
    n_rec = 2 * HEADS
    gate_rows = pltpu.VMEM((N_GATES, seq), F32)
    return pl.pallas_call(
        kernel,
        grid=(batch,),
        in_specs=[
            pl.BlockSpec((seq, 256), lambda b: (b, 0)),
            pl.BlockSpec((256, seq), lambda b: (0, b)),
            pl.BlockSpec((seq, LSTM_WIDTH), lambda b: (b, 0)),
            pl.BlockSpec((seq, LSTM_WIDTH), lambda b: (b, 0)),
            pl.BlockSpec((N_GATES, seq), lambda b: (0, b)),
            pl.BlockSpec((1, LSTM_WIDTH), lambda b: (0, 0)),
        ],
        out_specs=pl.BlockSpec((seq, LSTM_WIDTH), lambda b: (b, 0)),
        out_shape=jax.ShapeDtypeStruct((t, LSTM_WIDTH), BF16),
        scratch_shapes=[
            gate_rows,
            gate_rows,
            gate_rows,
            pltpu.VMEM((seq, LSTM_WIDTH), F32),
            pltpu.VMEM((n_rec, 256, L), BF16),
            pltpu.VMEM((n_rec, QK_DIM, 2 * V_DIM), F32),
            pltpu.VMEM((n_rec, 256, 2 * V_DIM), BF16),
            pltpu.VMEM((n_rec, 16, L), F32),
            pltpu.VMEM((n_rec, 16, 3 * L), F32),
        ],
        compiler_params=_params(("parallel",)),
        name="mlstm",
    )(lq, lkt, lv, lo, lgt, lstm_g)


def _outproj_kernel(att_ref, lstm_ref, x_ref, w_ref, g_ref, b_ref, o_ref):
    y = jnp.dot(att_ref[...], w_ref[0:ATT_WIDTH, :], preferred_element_type=F32)
    y = y + jnp.dot(lstm_ref[...], w_ref[ATT_WIDTH:ATT_WIDTH + LSTM_WIDTH, :], preferred_element_type=F32)
    o_ref[...] = _layer_norm(ALPHA * x_ref[...] + y, g_ref[...], b_ref[...])


def _outproj(att, lstm, x2d, w_out, g, b, tm):
    t = x2d.shape[0]
    row_blk = lambda n: pl.BlockSpec((tm, n), lambda i: (i, 0))
    full = lambda a: pl.BlockSpec(a.shape, lambda i: (0, 0))
    return pl.pallas_call(
        _outproj_kernel,
        grid=(t // tm,),
        in_specs=[row_blk(ATT_WIDTH), row_blk(LSTM_WIDTH), row_blk(D_MODEL), full(w_out), full(g), full(b)],
        out_specs=row_blk(D_MODEL),
        out_shape=jax.ShapeDtypeStruct((t, D_MODEL), F32),
        compiler_params=_params(("parallel",)),
        name="outproj_ln1",
    )(att, lstm, x2d, w_out, g, b)


HALO = 8


def _ffn_kernel(x_ref, xp_ref, xn_ref, wg_ref, wu_ref, cw_ref, cb_ref, wd_ref, g_ref, b_ref, o_ref,
                gs_ref, *, seq, tm):
    i = pl.program_id(0)
    xb = x_ref[...].astype(BF16)
    has_prev = ((i * tm) % seq != 0).astype(F32)
    has_next = (((i + 1) * tm) % seq != 0).astype(F32)
    halo = jnp.concatenate([xp_ref[...] * has_prev, xn_ref[...] * has_next], axis=0).astype(BF16)
    acc = None
    for j in range(len(FF_BOUNDS) - 1):
        cs = slice(FF_BOUNDS[j], FF_BOUNDS[j + 1])
        gs_ref[HALO:HALO + tm, cs] = jnp.dot(xb, wg_ref[:, cs], preferred_element_type=F32)
        gh = jnp.dot(halo, wg_ref[:, cs], preferred_element_type=F32)
        gs_ref[0:HALO, cs] = gh[0:HALO]
        gs_ref[HALO + tm:2 * HALO + tm, cs] = gh[HALO:2 * HALO]
        conv = (cb_ref[:, cs]
                + gs_ref[HALO - 1:HALO - 1 + tm, cs] * cw_ref[0:1, cs]
                + gs_ref[HALO:HALO + tm, cs] * cw_ref[1:2, cs]
                + gs_ref[HALO + 1:HALO + 1 + tm, cs] * cw_ref[2:3, cs])
        up = jnp.dot(xb, wu_ref[:, cs], preferred_element_type=F32)
        gelu = 0.5 * conv * (lax.erf(conv * (1.0 / math.sqrt(2.0))) + 1.0)
        hmid = (gelu * up).astype(BF16)
        part = jnp.dot(hmid, wd_ref[cs, :], preferred_element_type=F32)
        acc = part if acc is None else acc + part
    o_ref[...] = _layer_norm(ALPHA * x_ref[...] + acc, g_ref[...], b_ref[...])


def _ffn(x2d, w_g, w_u, conv_w, conv_b, w_d, g, b, seq, tm):
    t = x2d.shape[0]
    hb = tm // HALO
    last_hblk = t // HALO - 1
    kernel = functools.partial(_ffn_kernel, seq=seq, tm=tm)
    resident = lambda a: pl.BlockSpec(a.shape, lambda i: (0, 0), pipeline_mode=pl.Buffered(1))
    return pl.pallas_call(
        kernel,
        grid=(t // tm,),
        in_specs=[
            pl.BlockSpec((tm, D_MODEL), lambda i: (i, 0)),
            pl.BlockSpec((HALO, D_MODEL), lambda i: (jnp.maximum(i * hb - 1, 0), 0)),
            pl.BlockSpec((HALO, D_MODEL), lambda i: (jnp.minimum((i + 1) * hb, last_hblk), 0)),
            resident(w_g), resident(w_u), resident(conv_w), resident(conv_b), resident(w_d),
            resident(g), resident(b),
        ],
        out_specs=pl.BlockSpec((tm, D_MODEL), lambda i: (i, 0)),
        out_shape=jax.ShapeDtypeStruct((t, D_MODEL), F32),
        scratch_shapes=[
            pltpu.VMEM((tm + 2 * HALO, D_FF), F32),
        ],
        compiler_params=_params(("parallel",)),
        name="conv_ffn_ln2",
    )(x2d, x2d, x2d, w_g, w_u, conv_w, conv_b, w_d, g, b)


def _prep_layer(l, w_in, gate_bias, lam_q1, lam_k1, lam_q2, lam_k2, att_norm_g, lstm_norm_g, w_out,
                ln1_g, ln1_b, w_gu, conv_w, conv_b, w_down, ln2_g, ln2_b):
    wl = w_in[l]
    row = lambda a: a.reshape(1, -1).astype(F32)
    return dict(
        w_main=wl[:, :C_LG].astype(BF16),
        w_akt=wl[:, C_AK:C_AV].T.astype(BF16),
        w_kt=wl[:, C_LK:C_LV].T.astype(BF16),
        w_gt=wl[:, C_LG:C_END].T.astype(BF16),
        gbias=gate_bias[l].reshape(N_GATES, 1).astype(F32),
        lq1=row(lam_q1[l]), lk1=row(lam_k1[l]), lq2=row(lam_q2[l]), lk2=row(lam_k2[l]),
        att_g=row(att_norm_g[l]), lstm_g=row(lstm_norm_g[l]),
        w_out=w_out[l].astype(BF16), ln1_g=row(ln1_g[l]), ln1_b=row(ln1_b[l]),
        w_g=w_gu[l][:, :D_FF].astype(BF16), w_u=w_gu[l][:, D_FF:].astype(BF16),
        conv_w=conv_w[l].astype(F32), conv_b=row(conv_b[l]),
        w_d=w_down[l].astype(BF16), ln2_g=row(ln2_g[l]), ln2_b=row(ln2_b[l]),
    )


def _trunk(x, layers, tm_proj=PROJ_BLOCK, tm=TOKEN_BLOCK):
    batch, seq, _ = x.shape
    x2d = x.reshape(batch * seq, D_MODEL)
    for l, p in enumerate(layers):
        lam_init = 0.8 - 0.6 * math.exp(-0.3 * l)
        aq, akt, av, lq, lkt, lv, lo, lgt = _inproj(x2d, p["w_main"], p["w_akt"], p["w_kt"], p["w_gt"],
                                                    p["gbias"], tm_proj)
        att = _attention(aq, akt, av, p["lq1"], p["lk1"], p["lq2"], p["lk2"], p["att_g"], batch, seq, lam_init)
        lstm = _mlstm(lq, lkt, lv, lo, lgt, p["lstm_g"], batch, seq)
        x2d = _outproj(att, lstm, x2d, p["w_out"], p["ln1_g"], p["ln1_b"], tm_proj)
        x2d = _ffn(x2d, p["w_g"], p["w_u"], p["conv_w"], p["conv_b"], p["w_d"], p["ln2_g"], p["ln2_b"], seq, tm)
    return x2d.reshape(batch, seq, D_MODEL)


def kernel(x_prompt, x_sample, w_in, gate_bias, lam_q1, lam_k1, lam_q2, lam_k2, att_norm_g, lstm_norm_g,
           w_out, ln1_g, ln1_b, w_gu, conv_w, conv_b, w_down, ln2_g, ln2_b):
    layers = [_prep_layer(l, w_in, gate_bias, lam_q1, lam_k1, lam_q2, lam_k2, att_norm_g, lstm_norm_g,
                          w_out, ln1_g, ln1_b, w_gu, conv_w, conv_b, w_down, ln2_g, ln2_b)
              for l in range(DEPTH)]
    return (_trunk(x_prompt, layers), _trunk(x_sample, layers))
```
